```python
import jax
import jax.numpy as jnp
from jax import lax
import numpy as np

D_MODEL = 2048
BATCH = 2
SEQ = 8192
DEPTH = 2

MEM_LEN = 256
N_BRANCH = 4
MIX_WIDTH = D_MODEL // 4

NSA_HEAD_DIM = 64
NSA_HEADS = MIX_WIDTH // NSA_HEAD_DIM
NSA_KV_GROUPS = 2
NSA_GQA = NSA_HEADS // NSA_KV_GROUPS
NSA_N_BRANCH = 3
NSA_CMP_LEN = 32
NSA_CMP_STRIDE = 16
NSA_SEL_BLOCK = 64
NSA_N_SEL = 16
NSA_WINDOW = 512
NSA_Q_BLOCK = 128
NSA_FORCE_SCORE = 1e6

HGRN_HEAD_DIM = 128
HGRN_HEADS = MIX_WIDTH // HGRN_HEAD_DIM
HGRN_CHUNK = 64
HGRN_F_MIN = 1e-6

POOL_WINDOWS = (2, 4, 8, 16)
POOL_GROUPS = 4
POOL_GROUP_DIM = MIX_WIDTH // POOL_GROUPS

SG_CHUNK = 128
SG_HEADS = 4
SG_HEAD_DIM = MIX_WIDTH // SG_HEADS

X_HEADS = 4
X_HEAD_DIM = 128
X_WIDTH = X_HEADS * X_HEAD_DIM

D_FF = 5632
CONV_WIDTH = 3

LN_EPS = 1e-5
NEG_INF = -1e30
DEEPNORM_ALPHA = (2 * DEPTH) ** 0.25
DEEPNORM_BETA = (8 * DEPTH) ** -0.25

IN_SPLITS = (
    MIX_WIDTH,
    NSA_N_BRANCH * 2 * NSA_KV_GROUPS * NSA_HEAD_DIM,
    NSA_HEADS * NSA_N_BRANCH,
    4 * MIX_WIDTH,
    MIX_WIDTH,
    2 * MIX_WIDTH,
    N_BRANCH * D_MODEL,
)
W_IN_COLS = sum(IN_SPLITS)

kernel_name = 'hybrid_nsa_hgrn2_pool_sgu_deepnorm'


def layer_norm(x, g, b):
    xf = x.astype(jnp.float32)
    mu = jnp.mean(xf, -1, keepdims=True)
    var = jnp.mean(jnp.square(xf - mu), -1, keepdims=True)
    y = (xf - mu) * lax.rsqrt(var + LN_EPS) * g.astype(jnp.float32) + b.astype(jnp.float32)
    return y.astype(x.dtype)


def rms_norm(x, g):
    xf = x.astype(jnp.float32)
    y = xf * lax.rsqrt(jnp.mean(jnp.square(xf), -1, keepdims=True) + LN_EPS) * g.astype(jnp.float32)
    return y.astype(x.dtype)


def alibi_slopes(n_heads):
    return jnp.asarray(2.0 ** (-8.0 * np.arange(1, n_heads + 1) / n_heads), dtype=jnp.float32)


def masked_softmax(s, mask, axis=-1):
    p = jax.nn.softmax(jnp.where(mask, s, NEG_INF), axis=axis)
    return jnp.where(mask, p, 0.0)


def cmp_to_sel_matrix(n_cmp, n_blk):
    c0 = np.arange(n_cmp)[:, None] * NSA_CMP_STRIDE
    s0 = np.arange(n_blk)[None, :] * NSA_SEL_BLOCK
    ov = np.clip(np.minimum(c0 + NSA_CMP_LEN, s0 + NSA_SEL_BLOCK) - np.maximum(c0, s0), 0, None)
    return jnp.asarray(ov / NSA_CMP_LEN, dtype=jnp.float32)


def nsa_mixer(q, kv, gate_logits, cmp_pos, cmp_w):
    B, S, _ = q.shape
    G, J, dh, T = NSA_KV_GROUPS, NSA_GQA, NSA_HEAD_DIM, NSA_Q_BLOCK
    f32 = jnp.float32
    q = q.reshape(B, S, G, J, dh) * dh ** -0.5
    kv = kv.reshape(B, S, NSA_N_BRANCH, 2, G, dh)
    gates = jax.nn.sigmoid(gate_logits.reshape(B, S, G, J, NSA_N_BRANCH))
    slopes = alibi_slopes(NSA_HEADS).reshape(G, J)

    n_cmp = (S - NSA_CMP_LEN) // NSA_CMP_STRIDE + 1
    blk = np.arange(n_cmp)[:, None] * NSA_CMP_STRIDE + np.arange(NSA_CMP_LEN)[None, :]

    def compress(t, pos, w):
        tb = t[:, blk] + pos[None, None, :, None, :]
        return jnp.einsum('bclgd,lde->bcge', tb, w)

    k_cmp = compress(kv[:, :, 0, 0], cmp_pos[0], cmp_w[0])
    v_cmp = compress(kv[:, :, 0, 1], cmp_pos[1], cmp_w[1])
    cmp_end = jnp.arange(n_cmp) * NSA_CMP_STRIDE + (NSA_CMP_LEN - 1)

    n_blk = S // NSA_SEL_BLOCK
    n_sel = min(NSA_N_SEL, n_blk)
    sel_map = cmp_to_sel_matrix(n_cmp, n_blk)

    def to_blocks(t):
        return t.reshape(B, n_blk, NSA_SEL_BLOCK, G, dh).transpose(0, 1, 3, 2, 4)

    k_slc = to_blocks(kv[:, :, 1, 0])
    v_slc = to_blocks(kv[:, :, 1, 1])
    blk_ids = jnp.arange(n_blk)
    in_blk = jnp.arange(NSA_SEL_BLOCK)
    b_idx = jnp.arange(B)[:, None, None, None]
    g_idx = jnp.arange(G)[None, None, :, None]

    pad = jnp.zeros((B, NSA_WINDOW, G, dh), kv.dtype)
    k_win = jnp.concatenate([pad, kv[:, :, 2, 0]], axis=1)
    v_win = jnp.concatenate([pad, kv[:, :, 2, 1]], axis=1)

    def query_block(qb):
        q0 = qb * T
        t = q0 + jnp.arange(T)
        qt = lax.dynamic_slice_in_dim(q, q0, T, axis=1)
        gt = lax.dynamic_slice_in_dim(gates, q0, T, axis=1)
        sl = slopes[None, None, :, :, None]

        dist_c = (t[:, None] - cmp_end[None, :]).astype(f32)
        mask_c = (dist_c >= 0)[None, :, None, None, :]
        s_c = jnp.einsum('btgjd,bcgd->btgjc', qt, k_cmp).astype(f32) - sl * dist_c[None, :, None, None, :]
        p_c = masked_softmax(s_c, mask_c)
        o_c = jnp.einsum('btgjc,bcgd->btgjd', p_c.astype(v_cmp.dtype), v_cmp)

        imp = jnp.einsum('btgjc,cn->btgn', p_c, sel_map)
        cur = (t // NSA_SEL_BLOCK)[:, None]
        forced = (blk_ids[None] == 0) | (blk_ids[None] == cur) | (blk_ids[None] == cur - 1)
        future = blk_ids[None] * NSA_SEL_BLOCK > t[:, None]
        imp = jnp.where(forced[None, :, None], NSA_FORCE_SCORE,
                        jnp.where(future[None, :, None], NEG_INF, imp))
        _, idx = lax.top_k(imp, n_sel)
        k_s = k_slc[b_idx, idx, g_idx]
        v_s = v_slc[b_idx, idx, g_idx]
        pos_s = idx[..., None] * NSA_SEL_BLOCK + in_blk
        dist_s = (t[None, :, None, None, None] - pos_s).astype(f32)[:, :, :, None]
        mask_s = dist_s >= 0
        s_s = (jnp.einsum('btgjd,btgnkd->btgjnk', qt, k_s).astype(f32)
               - slopes[None, None, :, :, None, None] * dist_s)
        p_s = masked_softmax(s_s, mask_s, axis=(-2, -1))
        o_s = jnp.einsum('btgjnk,btgnkd->btgjd', p_s.astype(v_s.dtype), v_s)

        k_w = lax.dynamic_slice_in_dim(k_win, q0, T + NSA_WINDOW, axis=1)
        v_w = lax.dynamic_slice_in_dim(v_win, q0, T + NSA_WINDOW, axis=1)
        pos_w = q0 - NSA_WINDOW + jnp.arange(T + NSA_WINDOW)
        dist_w = t[:, None] - pos_w[None, :]
        mask_w = ((pos_w[None] >= 0) & (dist_w >= 0) & (dist_w < NSA_WINDOW))[None, :, None, None, :]
        s_w = (jnp.einsum('btgjd,bsgd->btgjs', qt, k_w).astype(f32)
               - sl * dist_w.astype(f32)[None, :, None, None, :])
        p_w = masked_softmax(s_w, mask_w)
        o_w = jnp.einsum('btgjs,bsgd->btgjd', p_w.astype(v_w.dtype), v_w)

        o = gt[..., 0:1] * o_c + gt[..., 1:2] * o_s + gt[..., 2:3] * o_w
        return o.reshape(B, T, NSA_HEADS * dh)

    out = lax.map(query_block, jnp.arange(S // T))
    return out.transpose(1, 0, 2, 3).reshape(B, S, NSA_HEADS * dh)


def hgrn2_mixer(q, f_logit, i, g, lb, norm_g):
    B, S, _ = q.shape
    H, dk, C = HGRN_HEADS, HGRN_HEAD_DIM, HGRN_CHUNK
    n = S // C
    f32 = jnp.float32
    lbh = lb.reshape(H, dk).astype(f32)
    z = f_logit.reshape(B, S, H, dk).astype(f32)
    f = jnp.maximum(lbh + (1.0 - lbh) * jax.nn.sigmoid(z), HGRN_F_MIN)
    log_f = jnp.log(f)
    k = 1.0 - f

    def to_chunks(a):
        return a.reshape(B, n, C, H, dk).transpose(1, 0, 3, 2, 4)

    xs = (to_chunks(q.astype(f32)), to_chunks(k), to_chunks(i.astype(f32)), to_chunks(log_f))
    causal = jnp.tril(jnp.ones((C, C), dtype=bool))[:, :, None]

    def step(state, inp):
        qq, kk, vv, ll = inp
        b = jnp.cumsum(ll, axis=2)
        o_inter = jnp.einsum('bhtk,bhkv->bhtv', qq * jnp.exp(b), state)
        decay = jnp.exp(jnp.where(causal, b[:, :, :, None, :] - b[:, :, None, :, :], NEG_INF))
        a = jnp.einsum('bhtk,bhtsk,bhsk->bhts', qq, decay, kk)
        o = o_inter + jnp.einsum('bhts,bhsv->bhtv', a, vv)
        b_last = b[:, :, -1:, :]
        state = (jnp.exp(b_last[:, :, 0, :, None]) * state
                 + jnp.einsum('bhsk,bhsv->bhkv', kk * jnp.exp(b_last - b), vv))
        return state, o

    s0 = jnp.zeros((B, H, dk, dk), f32)
    _, o = lax.scan(step, s0, xs)
    o = o.transpose(1, 0, 3, 2, 4).reshape(B, S, H, dk)
    o = rms_norm(o, norm_g.reshape(H, dk)).reshape(B, S, H * dk)
    return (o * jax.nn.silu(g.astype(f32))).astype(q.dtype)


def pool_mixer(c, w, scale):
    B, S, _ = c.shape
    cf = c.astype(jnp.float32).reshape(B, S, POOL_GROUPS, POOL_GROUP_DIM)
    cs = jnp.concatenate([jnp.zeros((B, 1, POOL_GROUPS, POOL_GROUP_DIM), jnp.float32),
                          jnp.cumsum(cf, axis=1)], axis=1)
    t = jnp.arange(S)
    outs = []
    for gi, win in enumerate(POOL_WINDOWS):
        lo = jnp.maximum(t + 1 - win, 0)
        cnt = jnp.minimum(t + 1, win).astype(jnp.float32)
        mean = (cs[:, 1:, gi] - cs[:, lo, gi]) / cnt[None, :, None]
        outs.append(mean - cf[:, :, gi])
    p = jnp.stack(outs, axis=2)
    y = jnp.einsum('bsgc,gcd->bsgd', p, w.astype(jnp.float32)).reshape(B, S, MIX_WIDTH)
    return (y * scale.astype(jnp.float32)).astype(c.dtype)


def sgu_mixer(u, v, ln_g, ln_b, ws, bs):
    B, S, _ = u.shape
    n = S // SG_CHUNK
    v = layer_norm(jax.nn.gelu(v), ln_g, ln_b).reshape(B, n, SG_CHUNK, SG_HEADS, SG_HEAD_DIM)
    mask = jnp.tril(jnp.ones((SG_CHUNK, SG_CHUNK), dtype=bool))
    wm = jnp.where(mask, ws, 0.0)
    vs = jnp.einsum('gqp,bnpgc->bnqgc', wm, v) + bs.T[None, None, :, :, None]
    return jax.nn.gelu(u) * vs.reshape(B, S, MIX_WIDTH)


def cross_attention(x, mem_n, wq, wk, wv, wo):
    B, S, _ = x.shape
    M = mem_n.shape[1]
    q = (x @ wq).reshape(B, S, X_HEADS, X_HEAD_DIM)
    k = (mem_n @ wk).reshape(B, M, X_HEADS, X_HEAD_DIM)
    v = (mem_n @ wv).reshape(B, M, X_HEADS, X_HEAD_DIM)
    s = jnp.einsum('bshd,bmhd->bhsm', q, k).astype(jnp.float32) * X_HEAD_DIM ** -0.5
    p = jax.nn.softmax(s, axis=-1).astype(v.dtype)
    o = jnp.einsum('bhsm,bmhd->bshd', p, v).reshape(B, S, X_WIDTH)
    return o @ wo


def conv_ffn(x, w_in, conv_w, conv_b, w_out):
    h = x @ w_in
    gate, up = jnp.split(h, 2, axis=-1)
    gate = lax.conv_general_dilated(
        gate, conv_w[:, None, :], window_strides=(1,), padding=((CONV_WIDTH - 1, 0),),
        dimension_numbers=('NWC', 'WIO', 'NWC'), feature_group_count=D_FF) + conv_b
    return (jax.nn.silu(gate) * up) @ w_out


def setup_inputs(seed: int = 0) -> dict:
    key = jax.random.key(seed)
    ks = iter(jax.random.split(key, 32))
    L, D = DEPTH, D_MODEL
    f32 = jnp.float32

    def nrm(shape, scale):
        return jax.random.normal(next(ks), shape, f32) * scale

    def gain(shape):
        return 1.0 + nrm(shape, 0.05)

    def bias(shape):
        return nrm(shape, 0.02)

    beta = DEEPNORM_BETA
    return {
        'x': nrm((BATCH, SEQ, D), 1.0),
        'mem': nrm((BATCH, MEM_LEN, D), 1.0),
        'w_in': nrm((L, D, W_IN_COLS), D ** -0.5),
        'nsa_cmp_pos': nrm((L, 2, NSA_CMP_LEN, NSA_HEAD_DIM), 0.1),
        'nsa_cmp_w': nrm((L, 2, NSA_CMP_LEN, NSA_HEAD_DIM, NSA_HEAD_DIM), (NSA_CMP_LEN * NSA_HEAD_DIM) ** -0.5),
        'hgrn_lb_logits': nrm((L, MIX_WIDTH), 1.0),
        'hgrn_norm_g': gain((L, MIX_WIDTH)),
        'pool_w': nrm((L, POOL_GROUPS, POOL_GROUP_DIM, POOL_GROUP_DIM), POOL_GROUP_DIM ** -0.5),
        'pool_scale': gain((L, MIX_WIDTH)),
        'sg_ln_g': gain((L, MIX_WIDTH)),
        'sg_ln_b': bias((L, MIX_WIDTH)),
        'sg_w': nrm((L, SG_HEADS, SG_CHUNK, SG_CHUNK), SG_CHUNK ** -0.5),
        'sg_b': gain((L, SG_HEADS, SG_CHUNK)),
        'w_branch': nrm((L, N_BRANCH, MIX_WIDTH, D), MIX_WIDTH ** -0.5),
        'w_mix_out': nrm((L, D, D), beta * D ** -0.5),
        'ln_mix_g': gain((L, D)),
        'ln_mix_b': bias((L, D)),
        'mem_ln_g': gain((D,)),
        'mem_ln_b': bias((D,)),
        'xattn_q': nrm((L, D, X_WIDTH), D ** -0.5),
        'xattn_k': nrm((L, D, X_WIDTH), D ** -0.5),
        'xattn_v': nrm((L, D, X_WIDTH), beta * D ** -0.5),
        'xattn_o': nrm((L, X_WIDTH, D), beta * X_WIDTH ** -0.5),
        'ln_x_g': gain((L, D)),
        'ln_x_b': bias((L, D)),
        'ffn_in': nrm((L, D, 2 * D_FF), D ** -0.5),
        'ffn_conv_w': nrm((L, CONV_WIDTH, D_FF), CONV_WIDTH ** -0.5),
        'ffn_conv_b': bias((L, D_FF)),
        'ffn_out': nrm((L, D_FF, D), beta * D_FF ** -0.5),
        'ln_ffn_g': gain((L, D)),
        'ln_ffn_b': bias((L, D)),
    }


def reference(x, mem, w_in, nsa_cmp_pos, nsa_cmp_w, hgrn_lb_logits, hgrn_norm_g, pool_w, pool_scale,
              sg_ln_g, sg_ln_b, sg_w, sg_b, w_branch, w_mix_out, ln_mix_g, ln_mix_b, mem_ln_g, mem_ln_b,
              xattn_q, xattn_k, xattn_v, xattn_o, ln_x_g, ln_x_b, ffn_in, ffn_conv_w, ffn_conv_b, ffn_out,
              ln_ffn_g, ln_ffn_b):
    B, S, D = x.shape
    mem_n = layer_norm(mem, mem_ln_g, mem_ln_b)
    sm = jax.nn.softmax(hgrn_lb_logits.astype(jnp.float32), axis=0)
    lower_bounds = jnp.cumsum(sm, axis=0) - sm[0]
    split_at = [int(v) for v in np.cumsum(IN_SPLITS)[:-1]]

    for l in range(DEPTH):
        h = x @ w_in[l]
        a_q, a_kv, a_g, b_in, c_in, d_in, merge = jnp.split(h, split_at, axis=-1)
        o_a = nsa_mixer(a_q, a_kv, a_g, nsa_cmp_pos[l], nsa_cmp_w[l])
        b_q, b_f, b_i, b_g = jnp.split(b_in, 4, axis=-1)
        o_b = hgrn2_mixer(b_q, b_f, b_i, b_g, lower_bounds[l], hgrn_norm_g[l])
        o_c = pool_mixer(c_in, pool_w[l], pool_scale[l])
        d_u, d_v = jnp.split(d_in, 2, axis=-1)
        o_d = sgu_mixer(d_u, d_v, sg_ln_g[l], sg_ln_b[l], sg_w[l], sg_b[l])
        branches = jnp.stack([o_a, o_b, o_c, o_d], axis=2)
        gates = jax.nn.sigmoid(merge.reshape(B, S, N_BRANCH, D))
        y = jnp.einsum('bsnw,nwd->bsnd', branches, w_branch[l])
        mixed = jnp.sum(gates * y, axis=2) @ w_mix_out[l]
        x = layer_norm(DEEPNORM_ALPHA * x + mixed, ln_mix_g[l], ln_mix_b[l])

        xa = cross_attention(x, mem_n, xattn_q[l], xattn_k[l], xattn_v[l], xattn_o[l])
        x = layer_norm(DEEPNORM_ALPHA * x + xa, ln_x_g[l], ln_x_b[l])

        ff = conv_ffn(x, ffn_in[l], ffn_conv_w[l], ffn_conv_b[l], ffn_out[l])
        x = layer_norm(DEEPNORM_ALPHA * x + ff, ln_ffn_g[l], ln_ffn_b[l])
    return x
```

```python
import functools

import numpy as np
import jax
import jax.numpy as jnp
from jax import lax
from jax.experimental import pallas as pl
from jax.experimental.pallas import tpu as pltpu

F32 = jnp.float32
BF16 = jnp.bfloat16

V7X_LANES = 128
V7X_VMEM_BYTES = 64 * 1024 * 1024
VMEM_LIMIT = 48 * 1024 * 1024

D_MODEL = 2048
DEPTH = 2
MIX = D_MODEL // 4
N_BRANCH = 4

NSA_DH = 64
NSA_G = 2
NSA_J = 4
NSA_HEADS = NSA_G * NSA_J
NSA_CMP_LEN = 32
NSA_CMP_STRIDE = 16
NSA_SEL_BLOCK = 64
NSA_N_SEL = 16
NSA_WINDOW = 512
NSA_T = 128
NSA_FORCE = 1e6

HG_H = 4
HG_DK = 128
HG_C = 64
HG_SUB = 16
HG_FMIN = 1e-6

POOL_WINDOWS = (2, 4, 8, 16)
POOL_HALO = 16

SG_CHUNK = 128
SG_H = 4

X_HEADS = 4
X_DH = 128

D_FF = 5632
CONV_W = 3
CONV_HALO = 16

LN_EPS = 1e-5
NEG = -1e30
ALPHA = (2 * DEPTH) ** 0.25

ALIBI = [float(2.0 ** (-8.0 * (h + 1) / NSA_HEADS)) for h in range(NSA_HEADS)]


def _params(*sem):
    return pltpu.CompilerParams(dimension_semantics=sem, vmem_limit_bytes=VMEM_LIMIT)


def _ln(y, g, b):
    mu = jnp.mean(y, axis=-1, keepdims=True)
    d = y - mu
    var = jnp.mean(d * d, axis=-1, keepdims=True)
    return d * lax.rsqrt(var + LN_EPS) * g + b


def _mm_kernel(a_ref, w_ref, o_ref):
    o_ref[...] = jnp.dot(a_ref[...], w_ref[...], preferred_element_type=F32).astype(o_ref.dtype)


def matmul(a, w, out_dtype, tm=512):
    M, K = a.shape
    N = w.shape[1]
    tm = min(tm, M)
    tn = min(N, 512)
    return pl.pallas_call(
        _mm_kernel,
        grid=(N // tn, M // tm),
        in_specs=[pl.BlockSpec((tm, K), lambda j, i: (i, 0)),
                  pl.BlockSpec((K, tn), lambda j, i: (0, j))],
        out_specs=pl.BlockSpec((tm, tn), lambda j, i: (i, j)),
        out_shape=jax.ShapeDtypeStruct((M, N), out_dtype),
        compiler_params=_params("parallel", "parallel"),
        name="matmul",
    )(a, w)


def _mm_ln_kernel(a_ref, w_ref, x_ref, g_ref, b_ref, o_ref, obf_ref, acc_ref):
    k = pl.program_id(1)

    @pl.when(k == 0)
    def _():
        acc_ref[...] = jnp.zeros_like(acc_ref)

    acc_ref[...] += jnp.dot(a_ref[...], w_ref[...], preferred_element_type=F32)

    @pl.when(k == pl.num_programs(1) - 1)
    def _():
        y = _ln(ALPHA * x_ref[...] + acc_ref[...], g_ref[...], b_ref[...])
        o_ref[...] = y
        obf_ref[...] = y.astype(BF16)


def matmul_residual_ln(a, w, x, g, b, tm=512, tk=512):
    M, K = a.shape
    D = w.shape[1]
    tk = min(tk, K)
    return pl.pallas_call(
        _mm_ln_kernel,
        grid=(M // tm, K // tk),
        in_specs=[pl.BlockSpec((tm, tk), lambda i, k: (i, k)),
                  pl.BlockSpec((tk, D), lambda i, k: (k, 0)),
                  pl.BlockSpec((tm, D), lambda i, k: (i, 0)),
                  pl.BlockSpec((1, D), lambda i, k: (0, 0)),
                  pl.BlockSpec((1, D), lambda i, k: (0, 0))],
        out_specs=[pl.BlockSpec((tm, D), lambda i, k: (i, 0)),
                   pl.BlockSpec((tm, D), lambda i, k: (i, 0))],
        out_shape=[jax.ShapeDtypeStruct((M, D), F32), jax.ShapeDtypeStruct((M, D), BF16)],
        scratch_shapes=[pltpu.VMEM((tm, D), F32)],
        compiler_params=_params("parallel", "arbitrary"),
        name="matmul_residual_ln",
    )(a, w, x, g.reshape(1, D), b.reshape(1, D))


def _ln_kernel(x_ref, g_ref, b_ref, o_ref):
    o_ref[...] = _ln(x_ref[...], g_ref[...], b_ref[...]).astype(o_ref.dtype)


def layer_norm_rows(x, g, b, out_dtype, tm=256):
    M, D = x.shape
    return pl.pallas_call(
        _ln_kernel,
        grid=(M // tm,),
        in_specs=[pl.BlockSpec((tm, D), lambda i: (i, 0)),
                  pl.BlockSpec((1, D), lambda i: (0, 0)),
                  pl.BlockSpec((1, D), lambda i: (0, 0))],
        out_specs=pl.BlockSpec((tm, D), lambda i: (i, 0)),
        out_shape=jax.ShapeDtypeStruct((M, D), out_dtype),
        compiler_params=_params("parallel"),
        name="layer_norm_rows",
    )(x, g.reshape(1, D), b.reshape(1, D))


def _merge_kernel(x_ref, oa_ref, ob_ref, oc_ref, od_ref, wm_ref, wb_ref, o_ref):
    x = x_ref[...]
    acc = None
    for n, o_n in enumerate((oa_ref, ob_ref, oc_ref, od_ref)):
        gate = jax.nn.sigmoid(jnp.dot(x, wm_ref[n], preferred_element_type=F32))
        y = jnp.dot(o_n[...], wb_ref[n], preferred_element_type=F32)
        acc = gate * y if acc is None else acc + gate * y
    o_ref[...] = acc.astype(o_ref.dtype)


def merge_branches(xbf, branches, w_merge, w_branch, tm=512, tn=512):
    M, D = xbf.shape
    W = branches[0].shape[1]
    return pl.pallas_call(
        _merge_kernel,
        grid=(D // tn, M // tm),
        in_specs=[pl.BlockSpec((tm, D), lambda j, i: (i, 0))]
        + [pl.BlockSpec((tm, W), lambda j, i: (i, 0))] * N_BRANCH
        + [pl.BlockSpec((N_BRANCH, D, tn), lambda j, i: (0, 0, j)),
           pl.BlockSpec((N_BRANCH, W, tn), lambda j, i: (0, 0, j))],
        out_specs=pl.BlockSpec((tm, tn), lambda j, i: (i, j)),
        out_shape=jax.ShapeDtypeStruct((M, D), BF16),
        compiler_params=_params("parallel", "parallel"),
        name="merge_branches",
    )(xbf, *branches, w_merge, w_branch)


def _compress_kernel(rows_ref, pos_ref, wlo_ref, whi_ref, o_ref):
    rows = rows_ref[0]
    lo = jnp.dot((rows + pos_ref[0:1]).astype(BF16), wlo_ref[...], preferred_element_type=F32)
    hi = jnp.dot((rows + pos_ref[1:2]).astype(BF16), whi_ref[...], preferred_element_type=F32)
    n = rows.shape[0]
    o_ref[0] = (lo + pltpu.roll(hi, n - 1, axis=0)).astype(o_ref.dtype)


def nsa_compress(kv_cmp, pos2, wlo, whi, B, S):
    R = S // NSA_CMP_STRIDE
    C = kv_cmp.shape[1]
    rows = kv_cmp.reshape(B, R, NSA_CMP_STRIDE * C)
    return pl.pallas_call(
        _compress_kernel,
        grid=(B,),
        in_specs=[pl.BlockSpec((1, R, NSA_CMP_STRIDE * C), lambda b: (b, 0, 0)),
                  pl.BlockSpec((2, NSA_CMP_STRIDE * C), lambda b: (0, 0)),
                  pl.BlockSpec((NSA_CMP_STRIDE * C, C), lambda b: (0, 0)),
                  pl.BlockSpec((NSA_CMP_STRIDE * C, C), lambda b: (0, 0))],
        out_specs=pl.BlockSpec((1, R, C), lambda b: (b, 0, 0)),
        out_shape=jax.ShapeDtypeStruct((B, R, C), BF16),
        compiler_params=_params("parallel"),
        name="nsa_compress",
    )(rows, pos2, wlo, whi)


def _nsa_kernel(q_ref, gl_ref, ks_ref, vs_ref, kw_ref, vw_ref, kc_ref, vc_ref, selmap_ref, o_ref, *, n_blk):
    T = NSA_T
    qb = pl.program_id(1)
    q0 = qb * T
    lane = lax.broadcasted_iota(jnp.int32, (T, T), 1)
    row = lax.broadcasted_iota(jnp.int32, (T, T), 0)
    t_pos = q0 + row
    lane4 = lax.broadcasted_iota(jnp.int32, (NSA_J * T, T), 1)
    row4 = lax.broadcasted_iota(jnp.int32, (NSA_J * T, 1), 0)

    q = q_ref[...]
    qall = jnp.concatenate([q[:, j * T:(j + 1) * T] for j in range(NSA_J)], axis=0)
    qall = (qall.astype(F32) * (NSA_DH ** -0.5))
    qpad = [jnp.where((lane4 // NSA_DH) == g, qall, 0.0).astype(BF16) for g in range(NSA_G)]
    slope_col = []
    for g in range(NSA_G):
        sc = jnp.zeros((NSA_J * T, 1), F32)
        for j in range(NSA_J):
            sc = jnp.where(row4 // T == j, ALIBI[g * NSA_J + j], sc)
        slope_col.append(sc)

    kc = kc_ref[0]
    vc = vc_ref[0]
    n_c = kc.shape[0]
    c_idx = lax.broadcasted_iota(jnp.int32, (T, n_c), 1)
    t_c = q0 + lax.broadcasted_iota(jnp.int32, (T, n_c), 0)
    dist_c = t_c - (c_idx * NSA_CMP_STRIDE + (NSA_CMP_LEN - 1))
    mask_c = dist_c >= 0
    dist_cf = dist_c.astype(F32)
    o_cmp = []
    imp = []
    for g in range(NSA_G):
        psum = jnp.zeros((T, n_c), F32)
        outs = []
        for j in range(NSA_J):
            qj = qpad[g][j * T:(j + 1) * T]
            s = lax.dot_general(qj, kc, (((1,), (1,)), ((), ())), preferred_element_type=F32)
            s = jnp.where(mask_c, s - ALIBI[g * NSA_J + j] * dist_cf, NEG)
            m = jnp.max(s, axis=-1, keepdims=True)
            p = jnp.where(mask_c, jnp.exp(s - m), 0.0)
            l = jnp.sum(p, axis=-1, keepdims=True)
            p = p / jnp.where(l > 0.0, l, 1.0)
            psum = psum + p
            outs.append(jnp.dot(p.astype(BF16), vc, preferred_element_type=F32))
        o_cmp.append(jnp.concatenate(outs, axis=0))
        imp.append(jnp.dot(psum, selmap_ref[...], preferred_element_type=F32,
                           precision=lax.Precision.HIGHEST))

    blk = lane
    cur = t_pos // NSA_SEL_BLOCK
    forced = (blk == 0) | (blk == cur) | (blk == cur - 1)
    future = blk * NSA_SEL_BLOCK > t_pos
    blk_f = blk.astype(F32)
    sel = []
    for g in range(NSA_G):
        work = jnp.where(forced, NSA_FORCE, jnp.where(future, NEG, imp[g]))
        if n_blk < T:
            work = jnp.where(blk >= n_blk, -jnp.inf, work)
        chosen = jnp.zeros((T, T), F32)
        for _ in range(min(NSA_N_SEL, n_blk)):
            mx = jnp.max(work, axis=-1, keepdims=True)
            first = jnp.min(jnp.where(work == mx, blk_f, float(T)), axis=-1, keepdims=True)
            hit = blk_f == first
            chosen = jnp.where(hit, 1.0, chosen)
            work = jnp.where(hit, -jnp.inf, work)
        sel.append(chosen.astype(BF16))

    def flash(k_ref, v_ref, lo, hi, neg_bias_fn):
        def body(kt, carry):
            k0 = pl.multiple_of(kt * T, T)
            k = k_ref[pl.ds(k0, T), :]
            v = v_ref[pl.ds(k0, T), :]
            dist = t_pos - (k0 + lane)
            rel = (k0 - q0 + lane[0:1, :]).astype(F32)
            new = []
            for g in range(NSA_G):
                m, l, acc = carry[g]
                negb = neg_bias_fn(g, kt, dist)
                s = lax.dot_general(qpad[g], k, (((1,), (1,)), ((), ())), preferred_element_type=F32)
                s = s + slope_col[g] * rel + jnp.concatenate([negb] * NSA_J, axis=0)
                m_new = jnp.maximum(m, jnp.max(s, axis=-1, keepdims=True))
                p = jnp.exp(s - m_new)
                a = jnp.exp(m - m_new)
                l = a * l + jnp.sum(p, axis=-1, keepdims=True)
                acc = a * acc + jnp.dot(p.astype(BF16), v, preferred_element_type=F32)
                new.append((m_new, l, acc))
            return tuple(new)

        init = tuple((jnp.full((NSA_J * T, 1), NEG, F32), jnp.zeros((NSA_J * T, 1), F32),
                      jnp.zeros((NSA_J * T, T), F32)) for _ in range(NSA_G))
        out = lax.fori_loop(lo, hi, body, init)
        return [acc / l for (_, l, acc) in out]

    def sel_bias(g, kt, dist):
        expand = (lax.broadcasted_iota(jnp.int32, (T, T), 0)
                  == 2 * kt + lax.broadcasted_iota(jnp.int32, (T, T), 1) // NSA_SEL_BLOCK)
        picked = jnp.dot(sel[g], expand.astype(BF16), preferred_element_type=F32)
        return jnp.where((picked > 0.5) & (dist >= 0), 0.0, NEG)

    def win_bias(g, kt, dist):
        return jnp.where((dist >= 0) & (dist < NSA_WINDOW), 0.0, NEG)

    o_sel = flash(ks_ref, vs_ref, 0, qb + 1, sel_bias)
    o_win = flash(kw_ref, vw_ref, jnp.maximum(qb - NSA_WINDOW // T, 0), qb + 1, win_bias)

    gates = jax.nn.sigmoid(gl_ref[...])
    res = []
    for g in range(NSA_G):
        cols = []
        for br in range(3):
            cols.append(jnp.concatenate(
                [jnp.broadcast_to(gates[:, (g * NSA_J + j) * 3 + br:(g * NSA_J + j) * 3 + br + 1], (T, T))
                 for j in range(NSA_J)], axis=0))
        res.append(cols[0] * o_cmp[g] + cols[1] * o_sel[g] + cols[2] * o_win[g])
    for j in range(NSA_J):
        o_ref[:, j * T:(j + 1) * T] = jnp.where(
            lane < NSA_DH, res[0][j * T:(j + 1) * T], res[1][j * T:(j + 1) * T]).astype(o_ref.dtype)


def nsa_attention(q, gate_logits, kv_sw, kvc, sel_map, B, S):
    nq = S // NSA_T
    R = kvc.shape[1]

    def seq_spec(c):
        return pl.BlockSpec((S, V7X_LANES), lambda b, i, c=c: (b, c))

    return pl.pallas_call(
        functools.partial(_nsa_kernel, n_blk=S // NSA_SEL_BLOCK),
        grid=(B, nq),
        in_specs=[pl.BlockSpec((NSA_T, MIX), lambda b, i: (b * nq + i, 0)),
                  pl.BlockSpec((NSA_T, V7X_LANES), lambda b, i: (b * nq + i, 0)),
                  seq_spec(0), seq_spec(1), seq_spec(2), seq_spec(3),
                  pl.BlockSpec((1, R, V7X_LANES), lambda b, i: (b, 0, 0)),
                  pl.BlockSpec((1, R, V7X_LANES), lambda b, i: (b, 0, 1)),
                  pl.BlockSpec((R, V7X_LANES), lambda b, i: (0, 0))],
        out_specs=pl.BlockSpec((NSA_T, MIX), lambda b, i: (b * nq + i, 0)),
        out_shape=jax.ShapeDtypeStruct((B * S, MIX), BF16),
        compiler_params=_params("parallel", "parallel"),
        name="nsa_attention",
    )(q, gate_logits, kv_sw, kv_sw, kv_sw, kv_sw, kvc, kvc, sel_map)


def _sel_map(S):
    n_c = S // NSA_CMP_STRIDE
    c0 = np.arange(n_c)[:, None] * NSA_CMP_STRIDE
    s0 = np.arange(V7X_LANES)[None, :] * NSA_SEL_BLOCK
    ov = np.clip(np.minimum(c0 + NSA_CMP_LEN, s0 + NSA_SEL_BLOCK) - np.maximum(c0, s0), 0, None)
    return jnp.asarray(ov / NSA_CMP_LEN, dtype=F32)


def _hgrn_kernel(x_ref, lb_ref, ng_ref, o_ref, state_ref):
    C, dk, U = HG_C, HG_DK, HG_SUB

    @pl.when(pl.program_id(1) == 0)
    def _():
        state_ref[...] = jnp.zeros_like(state_ref)

    tri = (lax.broadcasted_iota(jnp.int32, (C, C), 0) >= lax.broadcasted_iota(jnp.int32, (C, C), 1)).astype(F32)
    rows = lax.broadcasted_iota(jnp.int32, (U, 1), 0)
    for h in range(HG_H):
        def col(part):
            return x_ref[:, part * MIX + h * dk: part * MIX + (h + 1) * dk]
        q, z, v, gate = col(0), col(1), col(2), col(3)
        lb = lb_ref[:, h * dk:(h + 1) * dk]
        f = jnp.maximum(lb + (1.0 - lb) * jax.nn.sigmoid(z), HG_FMIN)
        kk = 1.0 - f
        b = jnp.dot(tri, jnp.log(f), preferred_element_type=F32, precision=lax.Precision.HIGHEST)
        st = state_ref[h]
        o = lax.dot_general((q * jnp.exp(b)).astype(BF16), st.astype(BF16), (((1,), (1,)), ((), ())),
                            preferred_element_type=F32)
        v_bf = v.astype(BF16)
        parts = []
        for i in range(C // U):
            r0 = i * U
            b_r, q_r, o_r = b[r0:r0 + U], q[r0:r0 + U], o[r0:r0 + U]
            if i > 0:
                b_e = b[r0 - 1:r0]
                qe = (q_r * jnp.exp(b_r - b_e)).astype(BF16)
                ke = (kk[0:r0] * jnp.exp(b_e - b[0:r0])).astype(BF16)
                a = lax.dot_general(qe, ke, (((1,), (1,)), ((), ())), preferred_element_type=F32)
                o_r = o_r + jnp.dot(a.astype(BF16), v_bf[0:r0], preferred_element_type=F32)
            for s in range(U):
                d = jnp.where(rows >= s, b_r - b[r0 + s:r0 + s + 1], NEG)
                a = jnp.sum(q_r * jnp.exp(d) * kk[r0 + s:r0 + s + 1], axis=-1, keepdims=True)
                o_r = o_r + a * v[r0 + s:r0 + s + 1]
            parts.append(o_r)
        o = jnp.concatenate(parts, axis=0)
        b_last = b[C - 1:C]
        kd = (kk * jnp.exp(b_last - b)).astype(BF16)
        state_ref[h] = st * jnp.exp(b_last) + jnp.dot(v.T.astype(BF16), kd, preferred_element_type=F32)
        y = o * lax.rsqrt(jnp.mean(o * o, axis=-1, keepdims=True) + LN_EPS) * ng_ref[:, h * dk:(h + 1) * dk]
        o_ref[:, h * dk:(h + 1) * dk] = (y * (gate * jax.nn.sigmoid(gate))).astype(o_ref.dtype)


def hgrn2(b_in, lb, norm_g, B, S):
    n = S // HG_C
    return pl.pallas_call(
        _hgrn_kernel,
        grid=(B, n),
        in_specs=[pl.BlockSpec((HG_C, 4 * MIX), lambda b, c: (b * n + c, 0)),
                  pl.BlockSpec((1, MIX), lambda b, c: (0, 0)),
                  pl.BlockSpec((1, MIX), lambda b, c: (0, 0))],
        out_specs=pl.BlockSpec((HG_C, MIX), lambda b, c: (b * n + c, 0)),
        out_shape=jax.ShapeDtypeStruct((B * S, MIX), BF16),
        scratch_shapes=[pltpu.VMEM((HG_H, HG_DK, HG_DK), F32)],
        compiler_params=_params("parallel", "arbitrary"),
        name="hgrn2",
    )(b_in, lb.reshape(1, MIX), norm_g.reshape(1, MIX))


def _pool_kernel(c_ref, halo_ref, w_ref, scale_ref, o_ref, ext_ref, *, tiles_per_seq):
    tm = c_ref.shape[0]
    first = pl.program_id(0) % tiles_per_seq == 0
    ext_ref[0:POOL_HALO, :] = jnp.where(first, 0.0, halo_ref[...])
    ext_ref[POOL_HALO:, :] = c_ref[...]
    t_in_seq = (pl.program_id(0) % tiles_per_seq) * tm + lax.broadcasted_iota(jnp.int32, (tm, 1), 0)
    for gi, win in enumerate(POOL_WINDOWS):
        lo, hi = gi * V7X_LANES, (gi + 1) * V7X_LANES
        tot = ext_ref[POOL_HALO:, lo:hi]
        for d in range(1, win):
            tot = tot + ext_ref[POOL_HALO - d:POOL_HALO - d + tm, lo:hi]
        cnt = jnp.minimum(t_in_seq + 1, win).astype(F32)
        p = tot / cnt - c_ref[:, lo:hi]
        y = jnp.dot(p.astype(BF16), w_ref[gi], preferred_element_type=F32)
        o_ref[:, lo:hi] = (y * scale_ref[:, lo:hi]).astype(o_ref.dtype)


def pool_mixer(c_in, w, scale, S, tm=512):
    M = c_in.shape[0]
    return pl.pallas_call(
        functools.partial(_pool_kernel, tiles_per_seq=S // tm),
        grid=(M // tm,),
        in_specs=[pl.BlockSpec((tm, MIX), lambda i: (i, 0)),
                  pl.BlockSpec((POOL_HALO, MIX), lambda i: (jnp.maximum(i * (tm // POOL_HALO) - 1, 0), 0)),
                  pl.BlockSpec((len(POOL_WINDOWS), V7X_LANES, V7X_LANES), lambda i: (0, 0, 0)),
                  pl.BlockSpec((1, MIX), lambda i: (0, 0))],
        out_specs=pl.BlockSpec((tm, MIX), lambda i: (i, 0)),
        out_shape=jax.ShapeDtypeStruct((M, MIX), BF16),
        scratch_shapes=[pltpu.VMEM((tm + POOL_HALO, MIX), F32)],
        compiler_params=_params("parallel"),
        name="pool_mixer",
    )(c_in, c_in, w, scale.reshape(1, MIX))


def _sgu_kernel(d_ref, g_ref, b_ref, ws_ref, bs_ref, o_ref):
    u = d_ref[:, 0:MIX]
    v = _ln(jax.nn.gelu(d_ref[:, MIX:2 * MIX]), g_ref[...], b_ref[...]).astype(BF16)
    causal = (lax.broadcasted_iota(jnp.int32, (SG_CHUNK, SG_CHUNK), 0)
              >= lax.broadcasted_iota(jnp.int32, (SG_CHUNK, SG_CHUNK), 1))
    hd = MIX // SG_H
    for g in range(SG_H):
        wm = jnp.where(causal, ws_ref[g], 0.0).astype(BF16)
        vs = jnp.dot(wm, v[:, g * hd:(g + 1) * hd], preferred_element_type=F32) + bs_ref[:, g:g + 1]
        o_ref[:, g * hd:(g + 1) * hd] = (jax.nn.gelu(u[:, g * hd:(g + 1) * hd]) * vs).astype(o_ref.dtype)


def sgu_mixer(d_in, ln_g, ln_b, ws, bs):
    M = d_in.shape[0]
    return pl.pallas_call(
        _sgu_kernel,
        grid=(M // SG_CHUNK,),
        in_specs=[pl.BlockSpec((SG_CHUNK, 2 * MIX), lambda i: (i, 0)),
                  pl.BlockSpec((1, MIX), lambda i: (0, 0)),
                  pl.BlockSpec((1, MIX), lambda i: (0, 0)),
                  pl.BlockSpec((SG_H, SG_CHUNK, SG_CHUNK), lambda i: (0, 0, 0)),
                  pl.BlockSpec((SG_CHUNK, SG_H), lambda i: (0, 0))],
        out_specs=pl.BlockSpec((SG_CHUNK, MIX), lambda i: (i, 0)),
        out_shape=jax.ShapeDtypeStruct((M, MIX), BF16),
        compiler_params=_params("parallel"),
        name="sgu_mixer",
    )(d_in, ln_g.reshape(1, MIX), ln_b.reshape(1, MIX), ws, bs.T)


def _xattn_kernel(q_ref, k_ref, v_ref, o_ref):
    for h in range(X_HEADS):
        sl = slice(h * X_DH, (h + 1) * X_DH)
        s = lax.dot_general(q_ref[:, sl], k_ref[:, sl], (((1,), (1,)), ((), ())),
                            preferred_element_type=F32) * (X_DH ** -0.5)
        p = jnp.exp(s - jnp.max(s, axis=-1, keepdims=True))
        p = p / jnp.sum(p, axis=-1, keepdims=True)
        o_ref[:, sl] = jnp.dot(p.astype(BF16), v_ref[:, sl], preferred_element_type=F32).astype(o_ref.dtype)


def cross_attention(q, k, v, S, mem_len, tm=512):
    M, W = q.shape
    return pl.pallas_call(
        _xattn_kernel,
        grid=(M // tm,),
        in_specs=[pl.BlockSpec((tm, W), lambda i: (i, 0)),
                  pl.BlockSpec((mem_len, W), lambda i: (i * tm // S, 0)),
                  pl.BlockSpec((mem_len, W), lambda i: (i * tm // S, 0))],
        out_specs=pl.BlockSpec((tm, W), lambda i: (i, 0)),
        out_shape=jax.ShapeDtypeStruct((M, W), BF16),
        compiler_params=_params("parallel"),
        name="cross_attention",
    )(q, k, v)


def _ffn_in_kernel(x_ref, halo_ref, wg_ref, wu_ref, cw_ref, cb_ref, o_ref, *, tiles_per_seq):
    first = pl.program_id(1) % tiles_per_seq == 0
    x = x_ref[...]
    g_main = jnp.dot(x, wg_ref[...], preferred_element_type=F32)
    g_halo = jnp.dot(halo_ref[...], wg_ref[...], preferred_element_type=F32)
    g_halo = jnp.where(first, 0.0, g_halo)
    ext = jnp.concatenate([g_halo, g_main], axis=0)
    n = ext.shape[0]
    conv = cb_ref[...] + cw_ref[CONV_W - 1:CONV_W] * g_main
    for d in range(1, CONV_W):
        conv = conv + cw_ref[CONV_W - 1 - d:CONV_W - d] * pltpu.roll(ext, d, axis=0)[CONV_HALO:n]
    up = jnp.dot(x, wu_ref[...], preferred_element_type=F32)
    o_ref[...] = (conv * jax.nn.sigmoid(conv) * up).astype(o_ref.dtype)


def ffn_in(xbf, w_gate, w_up, conv_w, conv_b, S, tm=512, tn=512):
    M, D = xbf.shape
    F = w_gate.shape[1]
    return pl.pallas_call(
        functools.partial(_ffn_in_kernel, tiles_per_seq=S // tm),
        grid=(F // tn, M // tm),
        in_specs=[pl.BlockSpec((tm, D), lambda j, i: (i, 0)),
                  pl.BlockSpec((CONV_HALO, D), lambda j, i: (jnp.maximum(i * (tm // CONV_HALO) - 1, 0), 0)),
                  pl.BlockSpec((D, tn), lambda j, i: (0, j)),
                  pl.BlockSpec((D, tn), lambda j, i: (0, j)),
                  pl.BlockSpec((CONV_W, tn), lambda j, i: (0, j)),
                  pl.BlockSpec((1, tn), lambda j, i: (0, j))],
        out_specs=pl.BlockSpec((tm, tn), lambda j, i: (i, j)),
        out_shape=jax.ShapeDtypeStruct((M, F), BF16),
        compiler_params=_params("parallel", "parallel"),
        name="ffn_in",
    )(xbf, xbf, w_gate, w_up, conv_w, conv_b.reshape(1, F))


def _compress_weights(cmp_pos, cmp_w):
    half = NSA_CMP_LEN // 2
    eye_g = jnp.eye(NSA_G, dtype=F32)
    ws, ps = [], []
    for part in range(2):
        blocks = []
        for kv in range(2):
            w = cmp_w[kv, part * half:(part + 1) * half]
            full = jnp.einsum('lde,gh->lgdhe', w, eye_g)
            z = jnp.zeros_like(full)
            pair = [full, z] if kv == 0 else [z, full]
            blocks.append(jnp.stack(pair, axis=3))
        w_all = jnp.stack(blocks, axis=1)
        ws.append(w_all.reshape(half * 2 * NSA_G * NSA_DH, 2 * NSA_G * NSA_DH).astype(BF16))
        p = cmp_pos[:, part * half:(part + 1) * half]
        p = jnp.broadcast_to(p.transpose(1, 0, 2)[:, :, None, :], (half, 2, NSA_G, NSA_DH))
        ps.append(p.reshape(-1))
    return jnp.stack(ps, axis=0), ws[0], ws[1]


def kernel(x, mem, w_in, nsa_cmp_pos, nsa_cmp_w, hgrn_lb_logits, hgrn_norm_g, pool_w, pool_scale, sg_ln_g, sg_ln_b, sg_w, sg_b, w_branch, w_mix_out, ln_mix_g, ln_mix_b, mem_ln_g, mem_ln_b, xattn_q, xattn_k, xattn_v, xattn_o, ln_x_g, ln_x_b, ffn_in_w, ffn_conv_w, ffn_conv_b, ffn_out, ln_ffn_g, ln_ffn_b):
    B, S, D = x.shape
    M = B * S
    mem_len = mem.shape[1]
    assert D == D_MODEL and S % 512 == 0 and S // NSA_SEL_BLOCK <= V7X_LANES

    sm = jax.nn.softmax(hgrn_lb_logits.astype(F32), axis=0)
    lower_bounds = jnp.cumsum(sm, axis=0) - sm[0]
    sel_map = _sel_map(S)
    mem_n = layer_norm_rows(mem.reshape(B * mem_len, D), mem_ln_g, mem_ln_b, BF16)

    xf = x.reshape(M, D)
    xbf = xf.astype(BF16)
    o0, o1, o2, o3, o4, o5 = np.cumsum([MIX, 768, 24, 4 * MIX, MIX, 2 * MIX])
    for l in range(DEPTH):
        w = w_in[l]
        w_q = w[:, :o0].reshape(D, NSA_G, NSA_J, NSA_DH).transpose(0, 2, 1, 3).reshape(D, MIX).astype(BF16)
        w_kvc = w[:, o0:o0 + 256].astype(BF16)
        w_kvsw = w[:, o0 + 256:o1].astype(BF16)
        w_ag = jnp.pad(w[:, o1:o2], ((0, 0), (0, V7X_LANES - 24))).astype(BF16)
        w_b = w[:, o2:o3].astype(BF16)
        w_c = w[:, o3:o4].astype(BF16)
        w_d = w[:, o4:o5].astype(BF16)
        w_merge = w[:, o5:].reshape(D, N_BRANCH, D).transpose(1, 0, 2).astype(BF16)

        a_q = matmul(xbf, w_q, BF16)
        kv_cmp = matmul(xbf, w_kvc, F32)
        kv_sw = matmul(xbf, w_kvsw, BF16)
        a_g = matmul(xbf, w_ag, F32)
        b_in = matmul(xbf, w_b, F32)
        c_in = matmul(xbf, w_c, F32)
        d_in = matmul(xbf, w_d, F32)

        pos2, wlo, whi = _compress_weights(nsa_cmp_pos[l], nsa_cmp_w[l])
        kvc = nsa_compress(kv_cmp, pos2, wlo, whi, B, S)
        o_a = nsa_attention(a_q, a_g, kv_sw, kvc, sel_map, B, S)
        o_b = hgrn2(b_in, lower_bounds[l], hgrn_norm_g[l], B, S)
        o_c = pool_mixer(c_in, pool_w[l].astype(BF16), pool_scale[l], S)
        o_d = sgu_mixer(d_in, sg_ln_g[l], sg_ln_b[l], sg_w[l], sg_b[l])

        wb = w_branch[l]
        wb_a = wb[0].reshape(NSA_G, NSA_J, NSA_DH, D).transpose(1, 0, 2, 3).reshape(MIX, D)
        wb = jnp.concatenate([wb_a[None], wb[1:]], axis=0).astype(BF16)
        gated = merge_branches(xbf, (o_a, o_b, o_c, o_d), w_merge, wb)
        xf, xbf = matmul_residual_ln(gated, w_mix_out[l].astype(BF16), xf, ln_mix_g[l], ln_mix_b[l])

        xq = matmul(xbf, xattn_q[l].astype(BF16), BF16)
        xk = matmul(mem_n, xattn_k[l].astype(BF16), BF16)
        xv = matmul(mem_n, xattn_v[l].astype(BF16), BF16)
        xo = cross_attention(xq, xk, xv, S, mem_len)
        xf, xbf = matmul_residual_ln(xo, xattn_o[l].astype(BF16), xf, ln_x_g[l], ln_x_b[l])

        wf = ffn_in_w[l]
        act = ffn_in(xbf, wf[:, :D_FF].astype(BF16), wf[:, D_FF:].astype(BF16), ffn_conv_w[l], ffn_conv_b[l], S)
        xf, xbf = matmul_residual_ln(act, ffn_out[l].astype(BF16), xf, ln_ffn_g[l], ln_ffn_b[l])
    return xf.reshape(B, S, D)
```

```python
import functools

import numpy as np
import jax
import jax.numpy as jnp
from jax import lax
from jax.experimental import pallas as pl
from jax.experimental.pallas import tpu as pltpu

F32 = jnp.float32
BF16 = jnp.bfloat16

V7X_LANES = 128
V7X_VMEM_BYTES = 64 * 1024 * 1024
VMEM_LIMIT = 48 * 1024 * 1024

D_MODEL = 2048
DEPTH = 2
MIX = D_MODEL // 4
N_BRANCH = 4

NSA_DH = 64
NSA_G = 2
NSA_J = 4
NSA_HEADS = NSA_G * NSA_J
NSA_CMP_LEN = 32
NSA_CMP_STRIDE = 16
NSA_SEL_BLOCK = 64
NSA_N_SEL = 16
NSA_WINDOW = 512
NSA_T = 128
NSA_TK = 256
NSA_FORCE = 1e6

HG_H = 4
HG_DK = 128
HG_C = 64
HG_SUB = 16
HG_FMIN = 1e-6

POOL_WINDOWS = (2, 4, 8, 16)
POOL_HALO = 16

SG_CHUNK = 128
SG_H = 4

X_HEADS = 4
X_DH = 128

D_FF = 5632
CONV_W = 3
CONV_HALO = 16

LN_EPS = 1e-5
NEG = -1e30
ALPHA = (2 * DEPTH) ** 0.25

ALIBI = [float(2.0 ** (-8.0 * (h + 1) / NSA_HEADS)) for h in range(NSA_HEADS)]


def _params(*sem):
    return pltpu.CompilerParams(dimension_semantics=sem, vmem_limit_bytes=VMEM_LIMIT)


def _ln(y, g, b):
    mu = jnp.mean(y, axis=-1, keepdims=True)
    d = y - mu
    var = jnp.mean(d * d, axis=-1, keepdims=True)
    return d * lax.rsqrt(var + LN_EPS) * g + b


def _mm_kernel(a_ref, w_ref, o_ref):
    o_ref[...] = jnp.dot(a_ref[...], w_ref[...], preferred_element_type=F32).astype(o_ref.dtype)


def matmul(a, w, out_dtype, tm=512):
    M, K = a.shape
    N = w.shape[1]
    tm = min(tm, M)
    tn = min(N, 512)
    return pl.pallas_call(
        _mm_kernel,
        grid=(N // tn, M // tm),
        in_specs=[pl.BlockSpec((tm, K), lambda j, i: (i, 0)),
                  pl.BlockSpec((K, tn), lambda j, i: (0, j))],
        out_specs=pl.BlockSpec((tm, tn), lambda j, i: (i, j)),
        out_shape=jax.ShapeDtypeStruct((M, N), out_dtype),
        compiler_params=_params("parallel", "parallel"),
        name="matmul",
    )(a, w)


def _mm_ln_kernel(a_ref, w_ref, x_ref, g_ref, b_ref, o_ref, obf_ref, acc_ref):
    k = pl.program_id(1)

    @pl.when(k == 0)
    def _():
        acc_ref[...] = jnp.zeros_like(acc_ref)

    acc_ref[...] += jnp.dot(a_ref[...], w_ref[...], preferred_element_type=F32)

    @pl.when(k == pl.num_programs(1) - 1)
    def _():
        y = _ln(ALPHA * x_ref[...] + acc_ref[...], g_ref[...], b_ref[...])
        o_ref[...] = y
        obf_ref[...] = y.astype(BF16)


def matmul_residual_ln(a, w, x, g, b, tm=512, tk=512):
    M, K = a.shape
    D = w.shape[1]
    tk = min(tk, K)
    return pl.pallas_call(
        _mm_ln_kernel,
        grid=(M // tm, K // tk),
        in_specs=[pl.BlockSpec((tm, tk), lambda i, k: (i, k)),
                  pl.BlockSpec((tk, D), lambda i, k: (k, 0)),
                  pl.BlockSpec((tm, D), lambda i, k: (i, 0)),
                  pl.BlockSpec((1, D), lambda i, k: (0, 0)),
                  pl.BlockSpec((1, D), lambda i, k: (0, 0))],
        out_specs=[pl.BlockSpec((tm, D), lambda i, k: (i, 0)),
                   pl.BlockSpec((tm, D), lambda i, k: (i, 0))],
        out_shape=[jax.ShapeDtypeStruct((M, D), F32), jax.ShapeDtypeStruct((M, D), BF16)],
        scratch_shapes=[pltpu.VMEM((tm, D), F32)],
        compiler_params=_params("parallel", "arbitrary"),
        name="matmul_residual_ln",
    )(a, w, x, g.reshape(1, D), b.reshape(1, D))


def _ln_kernel(x_ref, g_ref, b_ref, o_ref):
    o_ref[...] = _ln(x_ref[...], g_ref[...], b_ref[...]).astype(o_ref.dtype)


def layer_norm_rows(x, g, b, out_dtype, tm=256):
    M, D = x.shape
    return pl.pallas_call(
        _ln_kernel,
        grid=(M // tm,),
        in_specs=[pl.BlockSpec((tm, D), lambda i: (i, 0)),
                  pl.BlockSpec((1, D), lambda i: (0, 0)),
                  pl.BlockSpec((1, D), lambda i: (0, 0))],
        out_specs=pl.BlockSpec((tm, D), lambda i: (i, 0)),
        out_shape=jax.ShapeDtypeStruct((M, D), out_dtype),
        compiler_params=_params("parallel"),
        name="layer_norm_rows",
    )(x, g.reshape(1, D), b.reshape(1, D))


def _merge_kernel(x_ref, oa_ref, ob_ref, oc_ref, od_ref, wm_ref, wb_ref, o_ref):
    x = x_ref[...]
    acc = None
    for n, o_n in enumerate((oa_ref, ob_ref, oc_ref, od_ref)):
        gate = jax.nn.sigmoid(jnp.dot(x, wm_ref[n], preferred_element_type=F32))
        y = jnp.dot(o_n[...], wb_ref[n], preferred_element_type=F32)
        acc = gate * y if acc is None else acc + gate * y
    o_ref[...] = acc.astype(o_ref.dtype)


def merge_branches(xbf, branches, w_merge, w_branch, tm=512, tn=512):
    M, D = xbf.shape
    W = branches[0].shape[1]
    return pl.pallas_call(
        _merge_kernel,
        grid=(D // tn, M // tm),
        in_specs=[pl.BlockSpec((tm, D), lambda j, i: (i, 0))]
        + [pl.BlockSpec((tm, W), lambda j, i: (i, 0))] * N_BRANCH
        + [pl.BlockSpec((N_BRANCH, D, tn), lambda j, i: (0, 0, j)),
           pl.BlockSpec((N_BRANCH, W, tn), lambda j, i: (0, 0, j))],
        out_specs=pl.BlockSpec((tm, tn), lambda j, i: (i, j)),
        out_shape=jax.ShapeDtypeStruct((M, D), BF16),
        compiler_params=_params("parallel", "parallel"),
        name="merge_branches",
    )(xbf, *branches, w_merge, w_branch)


def _aux_lanes(k, lane, vals):
    for i, v in enumerate(vals):
        k = jnp.where(lane == NSA_DH + i, v, k)
    return k


def _compress_kernel(rows_ref, pos_ref, wlo_ref, whi_ref, kc_ref, vct_ref):
    rows = rows_ref[0]
    lo = jnp.dot((rows + pos_ref[0:1]).astype(BF16), wlo_ref[...], preferred_element_type=F32)
    hi = jnp.dot((rows + pos_ref[1:2]).astype(BF16), whi_ref[...], preferred_element_type=F32)
    n = rows.shape[0]
    res = lo + pltpu.roll(hi, n - 1, axis=0)
    kw = NSA_G * V7X_LANES
    c = lax.broadcasted_iota(jnp.int32, (n, kw), 0)
    lane = lax.broadcasted_iota(jnp.int32, (n, kw), 1) % V7X_LANES
    kc = _aux_lanes(res[:, :kw], lane, [(c // 8).astype(F32), (c % 8).astype(F32), 1.0, 1.0])
    kc_ref[0] = kc.astype(kc_ref.dtype)
    vct_ref[0] = res[:, kw:].T.astype(vct_ref.dtype)


def nsa_compress(kv_cmp, pos2, wlo, whi, B, S):
    R = S // NSA_CMP_STRIDE
    C = kv_cmp.shape[1]
    CO = wlo.shape[1]
    rows = kv_cmp.reshape(B, R, NSA_CMP_STRIDE * C)
    return pl.pallas_call(
        _compress_kernel,
        grid=(B,),
        in_specs=[pl.BlockSpec((1, R, NSA_CMP_STRIDE * C), lambda b: (b, 0, 0)),
                  pl.BlockSpec((2, NSA_CMP_STRIDE * C), lambda b: (0, 0)),
                  pl.BlockSpec((NSA_CMP_STRIDE * C, CO), lambda b: (0, 0)),
                  pl.BlockSpec((NSA_CMP_STRIDE * C, CO), lambda b: (0, 0))],
        out_specs=[pl.BlockSpec((1, R, NSA_G * V7X_LANES), lambda b: (b, 0, 0)),
                   pl.BlockSpec((1, V7X_LANES, R), lambda b: (b, 0, 0))],
        out_shape=[jax.ShapeDtypeStruct((B, R, NSA_G * V7X_LANES), BF16),
                   jax.ShapeDtypeStruct((B, V7X_LANES, R), BF16)],
        compiler_params=_params("parallel"),
        name="nsa_compress",
    )(rows, pos2, wlo, whi)


def _nsa_proj_kernel(x_ref, w_ref, q_ref, gl_ref, kvc_ref, k_ref, vt_ref, *, tiles_per_seq):
    tm = x_ref.shape[0]
    h = jnp.dot(x_ref[...], w_ref[...], preferred_element_type=F32)
    c0 = MIX
    c1 = c0 + V7X_LANES
    c2 = c1 + 2 * NSA_G * NSA_DH
    c3 = c2 + 2 * NSA_G * V7X_LANES
    q_ref[...] = h[:, :c0].astype(q_ref.dtype)
    gl_ref[...] = h[:, c0:c1]
    kvc_ref[...] = h[:, c1:c2]
    pos = (pl.program_id(0) % tiles_per_seq) * tm + lax.broadcasted_iota(jnp.int32, (tm, c3 - c2), 0)
    lane = lax.broadcasted_iota(jnp.int32, (tm, c3 - c2), 1) % V7X_LANES
    k = _aux_lanes(h[:, c2:c3], lane, [(pos // NSA_T).astype(F32), (pos % NSA_T).astype(F32), 1.0])
    k_ref[...] = k.astype(k_ref.dtype)
    vt_ref[...] = h[:, c3:].T.astype(vt_ref.dtype)


def nsa_project(xbf, w_all, S, tm=512):
    M, D = xbf.shape
    N = w_all.shape[1]
    kw = 2 * NSA_G * V7X_LANES
    return pl.pallas_call(
        functools.partial(_nsa_proj_kernel, tiles_per_seq=S // tm),
        grid=(M // tm,),
        in_specs=[pl.BlockSpec((tm, D), lambda i: (i, 0)),
                  pl.BlockSpec((D, N), lambda i: (0, 0))],
        out_specs=[pl.BlockSpec((tm, MIX), lambda i: (i, 0)),
                   pl.BlockSpec((tm, V7X_LANES), lambda i: (i, 0)),
                   pl.BlockSpec((tm, 2 * NSA_G * NSA_DH), lambda i: (i, 0)),
                   pl.BlockSpec((tm, kw), lambda i: (i, 0)),
                   pl.BlockSpec((2 * V7X_LANES, tm), lambda i: (0, i))],
        out_shape=[jax.ShapeDtypeStruct((M, MIX), BF16),
                   jax.ShapeDtypeStruct((M, V7X_LANES), F32),
                   jax.ShapeDtypeStruct((M, 2 * NSA_G * NSA_DH), F32),
                   jax.ShapeDtypeStruct((M, kw), BF16),
                   jax.ShapeDtypeStruct((2 * V7X_LANES, M), BF16)],
        compiler_params=_params("parallel"),
        name="nsa_project",
    )(xbf, w_all)


def _nsa_kernel(q_ref, gl_ref, ks_ref, vst_ref, kw_ref, vwt_ref, kc_ref, vct_ref, selmap_ref, o_ref,
                qt_ref, qtc_ref, selneg_ref, acc_ref, res_ref, *, n_blk):
    T, TK, DH = NSA_T, NSA_TK, NSA_DH
    qb = pl.program_id(1)
    q0 = qb * T
    t_row = q0 + lax.broadcasted_iota(jnp.int32, (1, T), 1)
    aux_row = lax.broadcasted_iota(jnp.int32, (DH, T), 0)

    qf = q_ref[...].astype(F32) * (DH ** -0.5)
    q_t = [qf[:, c * T:(c + 1) * T].T for c in range(NSA_HEADS // 2)]
    for g in range(NSA_G):
        main, cmp = [], []
        for j in range(NSA_J):
            h = g * NSA_J + j
            sl = ALIBI[h]
            off = (-sl * T) * qb.astype(F32)
            aux = jnp.where(aux_row == 0, sl * T, jnp.where(aux_row == 2, off, 0.0))
            aux_main = jnp.where(aux_row == 1, sl, aux)
            aux_cmp = jnp.where(aux_row == 1, sl * NSA_CMP_STRIDE,
                                jnp.where(aux_row == 3, sl * (NSA_CMP_LEN - 1), aux))
            qh = q_t[h // 2][(h % 2) * DH:(h % 2 + 1) * DH]
            main.append(jnp.concatenate([qh, aux_main], axis=0))
            cmp.append(jnp.concatenate([qh, aux_cmp], axis=0))
        qt_ref[g] = jnp.concatenate(main, axis=1).astype(BF16)
        qtc_ref[g] = jnp.concatenate(cmp, axis=1).astype(BF16)

    def per_head(x):
        return jnp.concatenate([x] * NSA_J, axis=1)

    n_c = kc_ref.shape[1]
    c_col = lax.broadcasted_iota(jnp.int32, (n_c, T), 0)
    negb_c = per_head(jnp.where(c_col * NSA_CMP_STRIDE + (NSA_CMP_LEN - 1) <= t_row, 0.0, NEG))
    has_key = per_head(t_row >= NSA_CMP_LEN - 1)
    blk = lax.broadcasted_iota(jnp.int32, (V7X_LANES, T), 0)
    blk_f = blk.astype(F32)
    cur = t_row // NSA_SEL_BLOCK
    forced = (blk == 0) | (blk == cur) | (blk == cur - 1)
    future = blk * NSA_SEL_BLOCK > t_row
    for g in range(NSA_G):
        kc = kc_ref[0, :, g * V7X_LANES:(g + 1) * V7X_LANES]
        vct = vct_ref[0, g * DH:(g + 1) * DH, :]
        s = jnp.dot(kc, qtc_ref[g], preferred_element_type=F32) + negb_c
        p = jnp.exp(s - jnp.max(s, axis=0, keepdims=True))
        l = jnp.sum(p, axis=0, keepdims=True)
        p = p * jnp.where(has_key, 1.0 / l, 0.0)
        res_ref[0, g] = jnp.dot(vct, p.astype(BF16), preferred_element_type=F32)
        psum = sum(p[:, j * T:(j + 1) * T] for j in range(NSA_J))
        imp = jnp.dot(selmap_ref[...], psum, preferred_element_type=F32, precision=lax.Precision.HIGHEST)
        work = jnp.where(forced, NSA_FORCE, jnp.where(future, NEG, imp))
        if n_blk < V7X_LANES:
            work = jnp.where(blk >= n_blk, -jnp.inf, work)
        selneg = jnp.full((V7X_LANES, T), NEG, F32)
        for _ in range(min(NSA_N_SEL, n_blk)):
            mx = jnp.max(work, axis=0, keepdims=True)
            first = jnp.min(jnp.where(work == mx, blk_f, float(V7X_LANES)), axis=0, keepdims=True)
            hit = blk_f == first
            selneg = jnp.where(hit, 0.0, selneg)
            work = jnp.where(hit, -jnp.inf, work)
        selneg_ref[g] = selneg

    key_in_tile = lax.broadcasted_iota(jnp.int32, (TK, T), 0)

    def tile(k_ref, vt_ref, kt, negb_fn, stats):
        k0 = pl.multiple_of(kt * TK, TK)
        s = [jnp.dot(k_ref[pl.ds(k0, TK), g * V7X_LANES:(g + 1) * V7X_LANES], qt_ref[g],
                     preferred_element_type=F32) + per_head(negb_fn(g, kt, k0)) for g in range(NSA_G)]
        new = []
        for g in range(NSA_G):
            m_old, l_old = stats[g]
            m_new = jnp.maximum(m_old, jnp.max(s[g], axis=0, keepdims=True))
            p = jnp.exp(s[g] - m_new)
            a = jnp.exp(m_old - m_new)
            vt = vt_ref[g * DH:(g + 1) * DH, pl.ds(k0, TK)]
            acc_ref[g] = a * acc_ref[g] + jnp.dot(vt, p.astype(BF16), preferred_element_type=F32)
            new.append((m_new, a * l_old + jnp.sum(p, axis=0, keepdims=True)))
        return tuple(new)

    def run(k_ref, vt_ref, lo, last, negb_loop, negb_last, branch):
        acc_ref[...] = jnp.zeros(acc_ref.shape, F32)
        init = tuple((jnp.full((1, NSA_J * T), NEG, F32), jnp.zeros((1, NSA_J * T), F32)) for _ in range(NSA_G))
        stats = lax.fori_loop(lo, last, lambda kt, st: tile(k_ref, vt_ref, kt, negb_loop, st), init)
        stats = tile(k_ref, vt_ref, last, negb_last, stats)
        for g in range(NSA_G):
            res_ref[branch, g] = acc_ref[g] / stats[g][1]

    def sel_negb(causal):
        def fn(g, kt, k0):
            per_tile = TK // NSA_SEL_BLOCK
            nb = jnp.concatenate(
                [jnp.broadcast_to(selneg_ref[g, pl.ds(kt * per_tile + i, 1), :], (NSA_SEL_BLOCK, T))
                 for i in range(per_tile)], axis=0)
            if causal:
                nb = jnp.where(k0 + key_in_tile <= t_row, nb, NEG)
            return nb
        return fn

    def win_negb(g, kt, k0):
        dist = t_row - (k0 + key_in_tile)
        return jnp.where((dist >= 0) & (dist < NSA_WINDOW), 0.0, NEG)

    last = qb // (TK // T)
    run(ks_ref, vst_ref, 0, last, sel_negb(False), sel_negb(True), 1)
    run(kw_ref, vwt_ref, jnp.maximum(q0 - (NSA_WINDOW - 1), 0) // TK, last, win_negb, win_negb, 2)

    g_t = jax.nn.sigmoid(gl_ref[...]).T
    for c in range(NSA_HEADS // 2):
        halves = []
        for e in range(2):
            h = 2 * c + e
            g, j = divmod(h, NSA_J)
            halves.append(sum(g_t[3 * h + br:3 * h + br + 1] * res_ref[br, g, :, j * T:(j + 1) * T]
                              for br in range(3)))
        o_ref[:, c * T:(c + 1) * T] = jnp.concatenate(halves, axis=0).T.astype(o_ref.dtype)


def nsa_attention(q, gate_logits, k_sw, vt_sw, kc, vct, sel_map_t, B, S):
    nq = S // NSA_T
    R = kc.shape[1]
    kw = NSA_G * V7X_LANES
    return pl.pallas_call(
        functools.partial(_nsa_kernel, n_blk=S // NSA_SEL_BLOCK),
        grid=(B, nq),
        in_specs=[pl.BlockSpec((NSA_T, MIX), lambda b, i: (b * nq + i, 0)),
                  pl.BlockSpec((NSA_T, V7X_LANES), lambda b, i: (b * nq + i, 0)),
                  pl.BlockSpec((S, kw), lambda b, i: (b, 0)),
                  pl.BlockSpec((V7X_LANES, S), lambda b, i: (0, b)),
                  pl.BlockSpec((S, kw), lambda b, i: (b, 1)),
                  pl.BlockSpec((V7X_LANES, S), lambda b, i: (1, b)),
                  pl.BlockSpec((1, R, kw), lambda b, i: (b, 0, 0)),
                  pl.BlockSpec((1, V7X_LANES, R), lambda b, i: (b, 0, 0)),
                  pl.BlockSpec((V7X_LANES, R), lambda b, i: (0, 0))],
        out_specs=pl.BlockSpec((NSA_T, MIX), lambda b, i: (b * nq + i, 0)),
        out_shape=jax.ShapeDtypeStruct((B * S, MIX), BF16),
        scratch_shapes=[pltpu.VMEM((NSA_G, V7X_LANES, NSA_J * NSA_T), BF16),
                        pltpu.VMEM((NSA_G, V7X_LANES, NSA_J * NSA_T), BF16),
                        pltpu.VMEM((NSA_G, V7X_LANES, NSA_T), F32),
                        pltpu.VMEM((NSA_G, NSA_DH, NSA_J * NSA_T), F32),
                        pltpu.VMEM((3, NSA_G, NSA_DH, NSA_J * NSA_T), F32)],
        compiler_params=_params("parallel", "parallel"),
        name="nsa_attention",
    )(q, gate_logits, k_sw, vt_sw, k_sw, vt_sw, kc, vct, sel_map_t)


def _sel_map_t(S):
    n_c = S // NSA_CMP_STRIDE
    c0 = np.arange(n_c)[None, :] * NSA_CMP_STRIDE
    s0 = np.arange(V7X_LANES)[:, None] * NSA_SEL_BLOCK
    ov = np.clip(np.minimum(c0 + NSA_CMP_LEN, s0 + NSA_SEL_BLOCK) - np.maximum(c0, s0), 0, None)
    return jnp.asarray(ov / NSA_CMP_LEN, dtype=F32)


def _hgrn_kernel(x_ref, lb_ref, ng_ref, o_ref, state_ref):
    C, dk, U = HG_C, HG_DK, HG_SUB

    @pl.when(pl.program_id(1) == 0)
    def _():
        state_ref[...] = jnp.zeros_like(state_ref)

    tri = (lax.broadcasted_iota(jnp.int32, (C, C), 0) >= lax.broadcasted_iota(jnp.int32, (C, C), 1)).astype(F32)
    rows = lax.broadcasted_iota(jnp.int32, (U, 1), 0)
    for h in range(HG_H):
        def col(part):
            return x_ref[:, part * MIX + h * dk: part * MIX + (h + 1) * dk]
        q, z, v, gate = col(0), col(1), col(2), col(3)
        lb = lb_ref[:, h * dk:(h + 1) * dk]
        f = jnp.maximum(lb + (1.0 - lb) * jax.nn.sigmoid(z), HG_FMIN)
        kk = 1.0 - f
        b = jnp.dot(tri, jnp.log(f), preferred_element_type=F32, precision=lax.Precision.HIGHEST)
        st = state_ref[h]
        o = lax.dot_general((q * jnp.exp(b)).astype(BF16), st.astype(BF16), (((1,), (1,)), ((), ())),
                            preferred_element_type=F32)
        v_bf = v.astype(BF16)
        parts = []
        for i in range(C // U):
            r0 = i * U
            b_r, q_r, o_r = b[r0:r0 + U], q[r0:r0 + U], o[r0:r0 + U]
            if i > 0:
                b_e = b[r0 - 1:r0]
                qe = (q_r * jnp.exp(b_r - b_e)).astype(BF16)
                ke = (kk[0:r0] * jnp.exp(b_e - b[0:r0])).astype(BF16)
                a = lax.dot_general(qe, ke, (((1,), (1,)), ((), ())), preferred_element_type=F32)
                o_r = o_r + jnp.dot(a.astype(BF16), v_bf[0:r0], preferred_element_type=F32)
            for s in range(U):
                d = jnp.where(rows >= s, b_r - b[r0 + s:r0 + s + 1], NEG)
                a = jnp.sum(q_r * jnp.exp(d) * kk[r0 + s:r0 + s + 1], axis=-1, keepdims=True)
                o_r = o_r + a * v[r0 + s:r0 + s + 1]
            parts.append(o_r)
        o = jnp.concatenate(parts, axis=0)
        b_last = b[C - 1:C]
        kd = (kk * jnp.exp(b_last - b)).astype(BF16)
        state_ref[h] = st * jnp.exp(b_last) + jnp.dot(v.T.astype(BF16), kd, preferred_element_type=F32)
        y = o * lax.rsqrt(jnp.mean(o * o, axis=-1, keepdims=True) + LN_EPS) * ng_ref[:, h * dk:(h + 1) * dk]
        o_ref[:, h * dk:(h + 1) * dk] = (y * (gate * jax.nn.sigmoid(gate))).astype(o_ref.dtype)


def hgrn2(b_in, lb, norm_g, B, S):
    n = S // HG_C
    return pl.pallas_call(
        _hgrn_kernel,
        grid=(B, n),
        in_specs=[pl.BlockSpec((HG_C, 4 * MIX), lambda b, c: (b * n + c, 0)),
                  pl.BlockSpec((1, MIX), lambda b, c: (0, 0)),
                  pl.BlockSpec((1, MIX), lambda b, c: (0, 0))],
        out_specs=pl.BlockSpec((HG_C, MIX), lambda b, c: (b * n + c, 0)),
        out_shape=jax.ShapeDtypeStruct((B * S, MIX), BF16),
        scratch_shapes=[pltpu.VMEM((HG_H, HG_DK, HG_DK), F32)],
        compiler_params=_params("parallel", "arbitrary"),
        name="hgrn2",
    )(b_in, lb.reshape(1, MIX), norm_g.reshape(1, MIX))


def _pool_kernel(c_ref, halo_ref, w_ref, scale_ref, o_ref, ext_ref, *, tiles_per_seq):
    tm = c_ref.shape[0]
    first = pl.program_id(0) % tiles_per_seq == 0
    ext_ref[0:POOL_HALO, :] = jnp.where(first, 0.0, halo_ref[...])
    ext_ref[POOL_HALO:, :] = c_ref[...]
    t_in_seq = (pl.program_id(0) % tiles_per_seq) * tm + lax.broadcasted_iota(jnp.int32, (tm, 1), 0)
    for gi, win in enumerate(POOL_WINDOWS):
        lo, hi = gi * V7X_LANES, (gi + 1) * V7X_LANES
        tot = ext_ref[POOL_HALO:, lo:hi]
        for d in range(1, win):
            tot = tot + ext_ref[POOL_HALO - d:POOL_HALO - d + tm, lo:hi]
        cnt = jnp.minimum(t_in_seq + 1, win).astype(F32)
        p = tot / cnt - c_ref[:, lo:hi]
        y = jnp.dot(p.astype(BF16), w_ref[gi], preferred_element_type=F32)
        o_ref[:, lo:hi] = (y * scale_ref[:, lo:hi]).astype(o_ref.dtype)


def pool_mixer(c_in, w, scale, S, tm=512):
    M = c_in.shape[0]
    return pl.pallas_call(
        functools.partial(_pool_kernel, tiles_per_seq=S // tm),
        grid=(M // tm,),
        in_specs=[pl.BlockSpec((tm, MIX), lambda i: (i, 0)),
                  pl.BlockSpec((POOL_HALO, MIX), lambda i: (jnp.maximum(i * (tm // POOL_HALO) - 1, 0), 0)),
                  pl.BlockSpec((len(POOL_WINDOWS), V7X_LANES, V7X_LANES), lambda i: (0, 0, 0)),
                  pl.BlockSpec((1, MIX), lambda i: (0, 0))],
        out_specs=pl.BlockSpec((tm, MIX), lambda i: (i, 0)),
        out_shape=jax.ShapeDtypeStruct((M, MIX), BF16),
        scratch_shapes=[pltpu.VMEM((tm + POOL_HALO, MIX), F32)],
        compiler_params=_params("parallel"),
        name="pool_mixer",
    )(c_in, c_in, w, scale.reshape(1, MIX))


def _sgu_kernel(d_ref, g_ref, b_ref, ws_ref, bs_ref, o_ref):
    u = d_ref[:, 0:MIX]
    v = _ln(jax.nn.gelu(d_ref[:, MIX:2 * MIX]), g_ref[...], b_ref[...]).astype(BF16)
    causal = (lax.broadcasted_iota(jnp.int32, (SG_CHUNK, SG_CHUNK), 0)
              >= lax.broadcasted_iota(jnp.int32, (SG_CHUNK, SG_CHUNK), 1))
    hd = MIX // SG_H
    for g in range(SG_H):
        wm = jnp.where(causal, ws_ref[g], 0.0).astype(BF16)
        vs = jnp.dot(wm, v[:, g * hd:(g + 1) * hd], preferred_element_type=F32) + bs_ref[:, g:g + 1]
        o_ref[:, g * hd:(g + 1) * hd] = (jax.nn.gelu(u[:, g * hd:(g + 1) * hd]) * vs).astype(o_ref.dtype)


def sgu_mixer(d_in, ln_g, ln_b, ws, bs):
    M = d_in.shape[0]
    return pl.pallas_call(
        _sgu_kernel,
        grid=(M // SG_CHUNK,),
        in_specs=[pl.BlockSpec((SG_CHUNK, 2 * MIX), lambda i: (i, 0)),
                  pl.BlockSpec((1, MIX), lambda i: (0, 0)),
                  pl.BlockSpec((1, MIX), lambda i: (0, 0)),
                  pl.BlockSpec((SG_H, SG_CHUNK, SG_CHUNK), lambda i: (0, 0, 0)),
                  pl.BlockSpec((SG_CHUNK, SG_H), lambda i: (0, 0))],
        out_specs=pl.BlockSpec((SG_CHUNK, MIX), lambda i: (i, 0)),
        out_shape=jax.ShapeDtypeStruct((M, MIX), BF16),
        compiler_params=_params("parallel"),
        name="sgu_mixer",
    )(d_in, ln_g.reshape(1, MIX), ln_b.reshape(1, MIX), ws, bs.T)


def _xattn_kernel(q_ref, k_ref, v_ref, o_ref):
    for h in range(X_HEADS):
        sl = slice(h * X_DH, (h + 1) * X_DH)
        s = lax.dot_general(q_ref[:, sl], k_ref[:, sl], (((1,), (1,)), ((), ())),
                            preferred_element_type=F32) * (X_DH ** -0.5)
        p = jnp.exp(s - jnp.max(s, axis=-1, keepdims=True))
        p = p / jnp.sum(p, axis=-1, keepdims=True)
        o_ref[:, sl] = jnp.dot(p.astype(BF16), v_ref[:, sl], preferred_element_type=F32).astype(o_ref.dtype)


def cross_attention(q, k, v, S, mem_len, tm=512):
    M, W = q.shape
    return pl.pallas_call(
        _xattn_kernel,
        grid=(M // tm,),
        in_specs=[pl.BlockSpec((tm, W), lambda i: (i, 0)),
                  pl.BlockSpec((mem_len, W), lambda i: (i * tm // S, 0)),
                  pl.BlockSpec((mem_len, W), lambda i: (i * tm // S, 0))],
        out_specs=pl.BlockSpec((tm, W), lambda i: (i, 0)),
        out_shape=jax.ShapeDtypeStruct((M, W), BF16),
        compiler_params=_params("parallel"),
        name="cross_attention",
    )(q, k, v)


def _ffn_in_kernel(x_ref, halo_ref, wg_ref, wu_ref, cw_ref, cb_ref, o_ref, *, tiles_per_seq):
    first = pl.program_id(1) % tiles_per_seq == 0
    x = x_ref[...]
    g_main = jnp.dot(x, wg_ref[...], preferred_element_type=F32)
    g_halo = jnp.dot(halo_ref[...], wg_ref[...], preferred_element_type=F32)
    g_halo = jnp.where(first, 0.0, g_halo)
    ext = jnp.concatenate([g_halo, g_main], axis=0)
    n = ext.shape[0]
    conv = cb_ref[...] + cw_ref[CONV_W - 1:CONV_W] * g_main
    for d in range(1, CONV_W):
        conv = conv + cw_ref[CONV_W - 1 - d:CONV_W - d] * pltpu.roll(ext, d, axis=0)[CONV_HALO:n]
    up = jnp.dot(x, wu_ref[...], preferred_element_type=F32)
    o_ref[...] = (conv * jax.nn.sigmoid(conv) * up).astype(o_ref.dtype)


def ffn_in(xbf, w_gate, w_up, conv_w, conv_b, S, tm=512, tn=512):
    M, D = xbf.shape
    F = w_gate.shape[1]
    return pl.pallas_call(
        functools.partial(_ffn_in_kernel, tiles_per_seq=S // tm),
        grid=(F // tn, M // tm),
        in_specs=[pl.BlockSpec((tm, D), lambda j, i: (i, 0)),
                  pl.BlockSpec((CONV_HALO, D), lambda j, i: (jnp.maximum(i * (tm // CONV_HALO) - 1, 0), 0)),
                  pl.BlockSpec((D, tn), lambda j, i: (0, j)),
                  pl.BlockSpec((D, tn), lambda j, i: (0, j)),
                  pl.BlockSpec((CONV_W, tn), lambda j, i: (0, j)),
                  pl.BlockSpec((1, tn), lambda j, i: (0, j))],
        out_specs=pl.BlockSpec((tm, tn), lambda j, i: (i, j)),
        out_shape=jax.ShapeDtypeStruct((M, F), BF16),
        compiler_params=_params("parallel", "parallel"),
        name="ffn_in",
    )(xbf, xbf, w_gate, w_up, conv_w, conv_b.reshape(1, F))


def _nsa_weights(w_q, w_kv, w_gate):
    D = w_q.shape[0]
    gap = jnp.zeros((D, V7X_LANES - NSA_DH), F32)
    per_branch = 2 * NSA_G * NSA_DH
    k_cols, v_cols = [], []
    for br in (1, 2):
        base = br * per_branch
        for g in range(NSA_G):
            k_cols += [w_kv[:, base + g * NSA_DH:base + (g + 1) * NSA_DH], gap]
        v_cols.append(w_kv[:, base + NSA_G * NSA_DH:base + per_branch])
    gate = jnp.pad(w_gate, ((0, 0), (0, V7X_LANES - w_gate.shape[1])))
    return jnp.concatenate([w_q, gate, w_kv[:, :per_branch]] + k_cols + v_cols, axis=1).astype(BF16)


def _compress_weights(cmp_pos, cmp_w):
    half = NSA_CMP_LEN // 2
    eye_g = jnp.eye(NSA_G, dtype=F32)
    ws, ps = [], []
    for part in range(2):
        blocks = []
        for kv in range(2):
            w = cmp_w[kv, part * half:(part + 1) * half]
            full = jnp.einsum('lde,gh->lgdhe', w, eye_g)
            z = jnp.zeros_like(full)
            pair = [full, z] if kv == 0 else [z, full]
            blocks.append(jnp.stack(pair, axis=3))
        w_all = jnp.stack(blocks, axis=1)
        w_all = w_all.reshape(half * 2 * NSA_G * NSA_DH, 2 * NSA_G * NSA_DH)
        gap = jnp.zeros((w_all.shape[0], V7X_LANES - NSA_DH), F32)
        w_all = jnp.concatenate([w_all[:, :NSA_DH], gap, w_all[:, NSA_DH:2 * NSA_DH], gap,
                                 w_all[:, 2 * NSA_DH:]], axis=1)
        ws.append(w_all.astype(BF16))
        p = cmp_pos[:, part * half:(part + 1) * half]
        p = jnp.broadcast_to(p.transpose(1, 0, 2)[:, :, None, :], (half, 2, NSA_G, NSA_DH))
        ps.append(p.reshape(-1))
    return jnp.stack(ps, axis=0), ws[0], ws[1]


def kernel(x, mem, w_in, nsa_cmp_pos, nsa_cmp_w, hgrn_lb_logits, hgrn_norm_g, pool_w, pool_scale, sg_ln_g, sg_ln_b, sg_w, sg_b, w_branch, w_mix_out, ln_mix_g, ln_mix_b, mem_ln_g, mem_ln_b, xattn_q, xattn_k, xattn_v, xattn_o, ln_x_g, ln_x_b, ffn_in_w, ffn_conv_w, ffn_conv_b, ffn_out, ln_ffn_g, ln_ffn_b):
    B, S, D = x.shape
    M = B * S
    mem_len = mem.shape[1]
    assert D == D_MODEL and S % 512 == 0 and S // NSA_SEL_BLOCK <= V7X_LANES

    sm = jax.nn.softmax(hgrn_lb_logits.astype(F32), axis=0)
    lower_bounds = jnp.cumsum(sm, axis=0) - sm[0]
    sel_map_t = _sel_map_t(S)
    mem_n = layer_norm_rows(mem.reshape(B * mem_len, D), mem_ln_g, mem_ln_b, BF16)

    xf = x.reshape(M, D)
    xbf = xf.astype(BF16)
    o0, o1, o2, o3, o4, o5 = np.cumsum([MIX, 768, 24, 4 * MIX, MIX, 2 * MIX])
    for l in range(DEPTH):
        w = w_in[l]
        w_nsa = _nsa_weights(w[:, :o0], w[:, o0:o1], w[:, o1:o2])
        w_b = w[:, o2:o3].astype(BF16)
        w_c = w[:, o3:o4].astype(BF16)
        w_d = w[:, o4:o5].astype(BF16)
        w_merge = w[:, o5:].reshape(D, N_BRANCH, D).transpose(1, 0, 2).astype(BF16)

        a_q, a_g, kv_cmp, k_sw, vt_sw = nsa_project(xbf, w_nsa, S)
        b_in = matmul(xbf, w_b, F32)
        c_in = matmul(xbf, w_c, F32)
        d_in = matmul(xbf, w_d, F32)

        pos2, wlo, whi = _compress_weights(nsa_cmp_pos[l], nsa_cmp_w[l])
        kc, vct = nsa_compress(kv_cmp, pos2, wlo, whi, B, S)
        o_a = nsa_attention(a_q, a_g, k_sw, vt_sw, kc, vct, sel_map_t, B, S)
        o_b = hgrn2(b_in, lower_bounds[l], hgrn_norm_g[l], B, S)
        o_c = pool_mixer(c_in, pool_w[l].astype(BF16), pool_scale[l], S)
        o_d = sgu_mixer(d_in, sg_ln_g[l], sg_ln_b[l], sg_w[l], sg_b[l])

        gated = merge_branches(xbf, (o_a, o_b, o_c, o_d), w_merge, w_branch[l].astype(BF16))
        xf, xbf = matmul_residual_ln(gated, w_mix_out[l].astype(BF16), xf, ln_mix_g[l], ln_mix_b[l])

        xq = matmul(xbf, xattn_q[l].astype(BF16), BF16)
        xk = matmul(mem_n, xattn_k[l].astype(BF16), BF16)
        xv = matmul(mem_n, xattn_v[l].astype(BF16), BF16)
        xo = cross_attention(xq, xk, xv, S, mem_len)
        xf, xbf = matmul_residual_ln(xo, xattn_o[l].astype(BF16), xf, ln_x_g[l], ln_x_b[l])

        wf = ffn_in_w[l]
        act = ffn_in(xbf, wf[:, :D_FF].astype(BF16), wf[:, D_FF:].astype(BF16), ffn_conv_w[l], ffn_conv_b[l], S)
        xf, xbf = matmul_residual_ln(act, ffn_out[l].astype(BF16), xf, ln_ffn_g[l], ln_ffn_b[l])
    return xf.reshape(B, S, D)
```

```python
import functools

import numpy as np
import jax
import jax.numpy as jnp
from jax import lax
from jax.experimental import pallas as pl
from jax.experimental.pallas import tpu as pltpu

F32 = jnp.float32
BF16 = jnp.bfloat16

V7X_LANES = 128
V7X_VMEM_BYTES = 64 * 1024 * 1024
VMEM_LIMIT = 48 * 1024 * 1024

D_MODEL = 2048
DEPTH = 2
MIX = D_MODEL // 4
N_BRANCH = 4

NSA_DH = 64
NSA_G = 2
NSA_J = 4
NSA_HEADS = NSA_G * NSA_J
NSA_CMP_LEN = 32
NSA_CMP_STRIDE = 16
NSA_SEL_BLOCK = 64
NSA_N_SEL = 16
NSA_WINDOW = 512
NSA_T = 128
NSA_TK = 512
NSA_LROWS = 16
NSA_FORCE = 1e6

HG_H = 4
HG_DK = 128
HG_C = 64
HG_SUB = 16
HG_FMIN = 1e-6

POOL_WINDOWS = (2, 4, 8, 16)
POOL_HALO = 16

SG_CHUNK = 128
SG_H = 4

X_HEADS = 4
X_DH = 128

D_FF = 5632
CONV_W = 3
CONV_HALO = 16

LN_EPS = 1e-5
NEG = -1e30
ALPHA = (2 * DEPTH) ** 0.25

ALIBI = [float(2.0 ** (-8.0 * (h + 1) / NSA_HEADS)) for h in range(NSA_HEADS)]


def _params(*sem):
    return pltpu.CompilerParams(dimension_semantics=sem, vmem_limit_bytes=VMEM_LIMIT)


def _ln(y, g, b):
    mu = jnp.mean(y, axis=-1, keepdims=True)
    d = y - mu
    var = jnp.mean(d * d, axis=-1, keepdims=True)
    return d * lax.rsqrt(var + LN_EPS) * g + b


def _mm_kernel(a_ref, w_ref, o_ref):
    o_ref[...] = jnp.dot(a_ref[...], w_ref[...], preferred_element_type=F32).astype(o_ref.dtype)


def matmul(a, w, out_dtype, tm=512):
    M, K = a.shape
    N = w.shape[1]
    tm = min(tm, M)
    tn = min(N, 512)
    return pl.pallas_call(
        _mm_kernel,
        grid=(N // tn, M // tm),
        in_specs=[pl.BlockSpec((tm, K), lambda j, i: (i, 0)),
                  pl.BlockSpec((K, tn), lambda j, i: (0, j))],
        out_specs=pl.BlockSpec((tm, tn), lambda j, i: (i, j)),
        out_shape=jax.ShapeDtypeStruct((M, N), out_dtype),
        compiler_params=_params("parallel", "parallel"),
        name="matmul",
    )(a, w)


def _mm_ln_kernel(a_ref, w_ref, x_ref, g_ref, b_ref, o_ref, obf_ref, *scratch, n_k):
    part = jnp.dot(a_ref[...], w_ref[...], preferred_element_type=F32)

    def finish(total):
        y = _ln(ALPHA * x_ref[...] + total, g_ref[...], b_ref[...])
        o_ref[...] = y
        obf_ref[...] = y.astype(BF16)

    if n_k == 1:
        finish(part)
        return
    acc_ref, = scratch
    k = pl.program_id(1)

    @pl.when(k == 0)
    def _():
        acc_ref[...] = part

    @pl.when((k > 0) & (k < n_k - 1))
    def _():
        acc_ref[...] += part

    @pl.when(k == n_k - 1)
    def _():
        finish(acc_ref[...] + part)


MM_LN_WEIGHT_TILE_BYTES = 16 * 1024 * 1024


def matmul_residual_ln(a, w, x, g, b, tm=512):
    M, K = a.shape
    D = w.shape[1]
    n_k = next(n for n in range(1, K) if K % n == 0 and (K // n) % V7X_LANES == 0
               and (K // n) * D * 2 * 2 <= MM_LN_WEIGHT_TILE_BYTES)
    tk = K // n_k
    return pl.pallas_call(
        functools.partial(_mm_ln_kernel, n_k=n_k),
        grid=(M // tm, n_k),
        in_specs=[pl.BlockSpec((tm, tk), lambda i, k: (i, k)),
                  pl.BlockSpec((tk, D), lambda i, k: (k, 0)),
                  pl.BlockSpec((tm, D), lambda i, k: (i, 0)),
                  pl.BlockSpec((1, D), lambda i, k: (0, 0)),
                  pl.BlockSpec((1, D), lambda i, k: (0, 0))],
        out_specs=[pl.BlockSpec((tm, D), lambda i, k: (i, 0)),
                   pl.BlockSpec((tm, D), lambda i, k: (i, 0))],
        out_shape=[jax.ShapeDtypeStruct((M, D), F32), jax.ShapeDtypeStruct((M, D), BF16)],
        scratch_shapes=[pltpu.VMEM((tm, D), F32)] if n_k > 1 else [],
        compiler_params=_params("parallel", "arbitrary"),
        name="matmul_residual_ln",
    )(a, w, x, g.reshape(1, D), b.reshape(1, D))


def _ln_kernel(x_ref, g_ref, b_ref, o_ref):
    o_ref[...] = _ln(x_ref[...], g_ref[...], b_ref[...]).astype(o_ref.dtype)


def layer_norm_rows(x, g, b, out_dtype, tm=256):
    M, D = x.shape
    return pl.pallas_call(
        _ln_kernel,
        grid=(M // tm,),
        in_specs=[pl.BlockSpec((tm, D), lambda i: (i, 0)),
                  pl.BlockSpec((1, D), lambda i: (0, 0)),
                  pl.BlockSpec((1, D), lambda i: (0, 0))],
        out_specs=pl.BlockSpec((tm, D), lambda i: (i, 0)),
        out_shape=jax.ShapeDtypeStruct((M, D), out_dtype),
        compiler_params=_params("parallel"),
        name="layer_norm_rows",
    )(x, g.reshape(1, D), b.reshape(1, D))


def _merge_kernel(x_ref, *refs):
    o_refs, wm_refs, (wb_ref, o_ref) = refs[:N_BRANCH], refs[N_BRANCH:2 * N_BRANCH], refs[2 * N_BRANCH:]
    x = x_ref[...]
    acc = None
    for n in range(N_BRANCH):
        gate = jax.nn.sigmoid(jnp.dot(x, wm_refs[n][...], preferred_element_type=F32))
        y = jnp.dot(o_refs[n][...], wb_ref[n], preferred_element_type=F32)
        acc = gate * y if acc is None else acc + gate * y
    o_ref[...] = acc.astype(o_ref.dtype)


def merge_branches(xbf, branches, w_merge, w_branch, tm=512, tn=512):
    M, D = xbf.shape
    W = branches[0].shape[1]
    return pl.pallas_call(
        _merge_kernel,
        grid=(D // tn, M // tm),
        in_specs=[pl.BlockSpec((tm, D), lambda j, i: (i, 0))]
        + [pl.BlockSpec((tm, W), lambda j, i: (i, 0))] * N_BRANCH
        + [pl.BlockSpec((D, tn), lambda j, i, n=n: (0, n * (D // tn) + j)) for n in range(N_BRANCH)]
        + [pl.BlockSpec((N_BRANCH, W, tn), lambda j, i: (0, 0, j))],
        out_specs=pl.BlockSpec((tm, tn), lambda j, i: (i, j)),
        out_shape=jax.ShapeDtypeStruct((M, D), BF16),
        compiler_params=_params("parallel", "parallel"),
        name="merge_branches",
    )(xbf, *branches, *([w_merge] * N_BRANCH), w_branch)


def _aux_lanes(k, lane, vals):
    for i, v in enumerate(vals):
        k = jnp.where(lane == NSA_DH + i, v, k)
    return k


def _compress_kernel(rows_ref, pos_ref, wlo_ref, whi_ref, kc_ref, vct_ref):
    rows = rows_ref[0]
    lo = jnp.dot((rows + pos_ref[0:1]).astype(BF16), wlo_ref[...], preferred_element_type=F32)
    hi = jnp.dot((rows + pos_ref[1:2]).astype(BF16), whi_ref[...], preferred_element_type=F32)
    n = rows.shape[0]
    res = lo + pltpu.roll(hi, n - 1, axis=0)
    kw = NSA_G * V7X_LANES
    c = lax.broadcasted_iota(jnp.int32, (n, kw), 0)
    lane = lax.broadcasted_iota(jnp.int32, (n, kw), 1) % V7X_LANES
    kc = _aux_lanes(res[:, :kw], lane, [(c // 8).astype(F32), (c % 8).astype(F32), 1.0, 1.0])
    kc_ref[0] = kc.astype(kc_ref.dtype)
    vct_ref[0] = res[:, kw:].T.astype(vct_ref.dtype)


def nsa_compress(kv_cmp, pos2, wlo, whi, B, S):
    R = S // NSA_CMP_STRIDE
    C = kv_cmp.shape[1]
    CO = wlo.shape[1]
    rows = kv_cmp.reshape(B, R, NSA_CMP_STRIDE * C)
    return pl.pallas_call(
        _compress_kernel,
        grid=(B,),
        in_specs=[pl.BlockSpec((1, R, NSA_CMP_STRIDE * C), lambda b: (b, 0, 0)),
                  pl.BlockSpec((2, NSA_CMP_STRIDE * C), lambda b: (0, 0)),
                  pl.BlockSpec((NSA_CMP_STRIDE * C, CO), lambda b: (0, 0)),
                  pl.BlockSpec((NSA_CMP_STRIDE * C, CO), lambda b: (0, 0))],
        out_specs=[pl.BlockSpec((1, R, NSA_G * V7X_LANES), lambda b: (b, 0, 0)),
                   pl.BlockSpec((1, V7X_LANES, R), lambda b: (b, 0, 0))],
        out_shape=[jax.ShapeDtypeStruct((B, R, NSA_G * V7X_LANES), BF16),
                   jax.ShapeDtypeStruct((B, V7X_LANES, R), BF16)],
        compiler_params=_params("parallel"),
        name="nsa_compress",
    )(rows, pos2, wlo, whi)


def _nsa_proj_kernel(x_ref, w_ref, q_ref, gl_ref, kvc_ref, k_ref, vt_ref, *, tiles_per_seq):
    tm = x_ref.shape[0]
    h = jnp.dot(x_ref[...], w_ref[...], preferred_element_type=F32)
    c0 = MIX
    c1 = c0 + V7X_LANES
    c2 = c1 + 2 * NSA_G * NSA_DH
    c3 = c2 + 2 * NSA_G * V7X_LANES
    q_ref[...] = h[:, :c0].astype(q_ref.dtype)
    gl_ref[...] = h[:, c0:c1]
    kvc_ref[...] = h[:, c1:c2]
    pos = (pl.program_id(0) % tiles_per_seq) * tm + lax.broadcasted_iota(jnp.int32, (tm, c3 - c2), 0)
    lane = lax.broadcasted_iota(jnp.int32, (tm, c3 - c2), 1) % V7X_LANES
    k = _aux_lanes(h[:, c2:c3], lane, [(pos // NSA_T).astype(F32), (pos % NSA_T).astype(F32), 1.0])
    k_ref[...] = k.astype(k_ref.dtype)
    vt_ref[...] = h[:, c3:].T.astype(vt_ref.dtype)


def nsa_project(xbf, w_all, S, tm=512):
    M, D = xbf.shape
    N = w_all.shape[1]
    kw = 2 * NSA_G * V7X_LANES
    return pl.pallas_call(
        functools.partial(_nsa_proj_kernel, tiles_per_seq=S // tm),
        grid=(M // tm,),
        in_specs=[pl.BlockSpec((tm, D), lambda i: (i, 0)),
                  pl.BlockSpec((D, N), lambda i: (0, 0))],
        out_specs=[pl.BlockSpec((tm, MIX), lambda i: (i, 0)),
                   pl.BlockSpec((tm, V7X_LANES), lambda i: (i, 0)),
                   pl.BlockSpec((tm, 2 * NSA_G * NSA_DH), lambda i: (i, 0)),
                   pl.BlockSpec((tm, kw), lambda i: (i, 0)),
                   pl.BlockSpec((2 * V7X_LANES, tm), lambda i: (0, i))],
        out_shape=[jax.ShapeDtypeStruct((M, MIX), BF16),
                   jax.ShapeDtypeStruct((M, V7X_LANES), F32),
                   jax.ShapeDtypeStruct((M, 2 * NSA_G * NSA_DH), F32),
                   jax.ShapeDtypeStruct((M, kw), BF16),
                   jax.ShapeDtypeStruct((2 * V7X_LANES, M), BF16)],
        compiler_params=_params("parallel"),
        name="nsa_project",
    )(xbf, w_all)


def _nsa_kernel(q_ref, gl_ref, ks_ref, vst_ref, kw_ref, vwt_ref, kc_ref, vct_ref, selmap_ref, o_ref,
                qt_ref, qtc_ref, selneg_ref, p_ref, acc_ref, res_ref, *, n_blk):
    T, TK, DH = NSA_T, NSA_TK, NSA_DH
    qb = pl.program_id(1)
    q0 = qb * T
    t_row = q0 + lax.broadcasted_iota(jnp.int32, (1, T), 1)
    aux_row = lax.broadcasted_iota(jnp.int32, (DH, T), 0)

    qf = q_ref[...].astype(F32) * (DH ** -0.5)
    q_t = [qf[:, c * T:(c + 1) * T].T for c in range(NSA_HEADS // 2)]
    for g in range(NSA_G):
        main, cmp = [], []
        for j in range(NSA_J):
            h = g * NSA_J + j
            sl = ALIBI[h]
            off = (-sl * T) * qb.astype(F32)
            aux = jnp.where(aux_row == 0, sl * T, jnp.where(aux_row == 2, off, 0.0))
            aux_main = jnp.where(aux_row == 1, sl, aux)
            aux_cmp = jnp.where(aux_row == 1, sl * NSA_CMP_STRIDE,
                                jnp.where(aux_row == 3, sl * (NSA_CMP_LEN - 1), aux))
            qh = q_t[h // 2][(h % 2) * DH:(h % 2 + 1) * DH]
            main.append(jnp.concatenate([qh, aux_main], axis=0))
            cmp.append(jnp.concatenate([qh, aux_cmp], axis=0))
        qt_ref[g] = jnp.concatenate(main, axis=1).astype(BF16)
        qtc_ref[g] = jnp.concatenate(cmp, axis=1).astype(BF16)

    def per_head(x):
        return jnp.concatenate([x] * NSA_J, axis=1)

    n_c = kc_ref.shape[1]
    c_col = lax.broadcasted_iota(jnp.int32, (n_c, T), 0)
    negb_c = per_head(jnp.where(c_col * NSA_CMP_STRIDE + (NSA_CMP_LEN - 1) <= t_row, 0.0, NEG))
    has_key = per_head(t_row >= NSA_CMP_LEN - 1)
    blk = lax.broadcasted_iota(jnp.int32, (V7X_LANES, T), 0)
    blk_f = blk.astype(F32)
    cur = t_row // NSA_SEL_BLOCK
    forced = (blk == 0) | (blk == cur) | (blk == cur - 1)
    future = blk * NSA_SEL_BLOCK > t_row
    for g in range(NSA_G):
        kc = kc_ref[0, :, g * V7X_LANES:(g + 1) * V7X_LANES]
        vct = vct_ref[0, g * DH:(g + 1) * DH, :]
        s = jnp.dot(kc, qtc_ref[g], preferred_element_type=F32) + negb_c
        p = jnp.exp(s - jnp.max(s, axis=0, keepdims=True))
        l = jnp.sum(p, axis=0, keepdims=True)
        p = p * jnp.where(has_key, 1.0 / l, 0.0)
        res_ref[0, g] = jnp.dot(vct, p.astype(BF16), preferred_element_type=F32)
        psum = sum(p[:, j * T:(j + 1) * T] for j in range(NSA_J))
        imp = jnp.dot(selmap_ref[...], psum, preferred_element_type=F32, precision=lax.Precision.HIGHEST)
        work = jnp.where(forced, NSA_FORCE, jnp.where(future, NEG, imp))
        if n_blk < V7X_LANES:
            work = jnp.where(blk >= n_blk, -jnp.inf, work)
        selneg = jnp.full((V7X_LANES, T), NEG, F32)
        for _ in range(min(NSA_N_SEL, n_blk)):
            mx = jnp.max(work, axis=0, keepdims=True)
            first = jnp.min(jnp.where(work == mx, blk_f, float(V7X_LANES)), axis=0, keepdims=True)
            hit = blk_f == first
            selneg = jnp.where(hit, 0.0, selneg)
            work = jnp.where(hit, -jnp.inf, work)
        selneg_ref[g] = selneg

    key_in_tile = lax.broadcasted_iota(jnp.int32, (TK, T), 0)
    ones_rows = jnp.ones((NSA_LROWS, TK), BF16)
    last = qb // (TK // T)

    def flash(k_ref, vt_ref, lo, negb_loop, negb_last, branch):
        def scores(kt, negb_fn):
            k0 = pl.multiple_of(kt * TK, TK)
            return [jnp.dot(k_ref[pl.ds(k0, TK), g * V7X_LANES:(g + 1) * V7X_LANES], qt_ref[g],
                            preferred_element_type=F32) + per_head(negb_fn(g, kt, k0)) for g in range(NSA_G)]

        def fold_values(kt, a):
            k0 = pl.multiple_of(kt * TK, TK)
            for g in range(NSA_G):
                vt = jnp.concatenate([vt_ref[g * DH:(g + 1) * DH, pl.ds(k0, TK)], ones_rows], axis=0)
                acc_ref[g] = a[g] * acc_ref[g] + jnp.dot(vt, p_ref[g], preferred_element_type=F32)

        def softmax(s, m_old):
            m_new, a = [], []
            for g in range(NSA_G):
                m_new.append(jnp.maximum(m_old[g], jnp.max(s[g], axis=0, keepdims=True)))
                p_ref[g] = jnp.exp((s[g] - m_new[g]).astype(BF16))
                a.append(jnp.exp(m_old[g] - m_new[g]))
            return tuple(m_new), tuple(a)

        def step(kt, carry, negb_fn):
            m, a = carry
            s = scores(kt, negb_fn)
            fold_values(kt - 1, a)
            return softmax(s, m)

        acc_ref[...] = jnp.zeros(acc_ref.shape, F32)
        m_init = tuple(jnp.full((1, NSA_J * T), NEG, F32) for _ in range(NSA_G))
        carry = softmax(scores(lo, negb_last), m_init)
        carry = lax.fori_loop(lo + 1, last, lambda kt, c: step(kt, c, negb_loop), carry)
        carry = lax.cond(last > lo, lambda c: step(last, c, negb_last), lambda c: c, carry)
        fold_values(last, carry[1])
        for g in range(NSA_G):
            res_ref[branch, g] = acc_ref[g, :DH] / acc_ref[g, DH:DH + 1]

    def sel_negb(causal):
        def fn(g, kt, k0):
            per_tile = TK // NSA_SEL_BLOCK
            nb = jnp.concatenate(
                [jnp.broadcast_to(selneg_ref[g, pl.ds(kt * per_tile + i, 1), :], (NSA_SEL_BLOCK, T))
                 for i in range(per_tile)], axis=0)
            if causal:
                nb = jnp.where(k0 + key_in_tile <= t_row, nb, NEG)
            return nb
        return fn

    def win_negb(g, kt, k0):
        dist = t_row - (k0 + key_in_tile)
        return jnp.where((dist >= 0) & (dist < NSA_WINDOW), 0.0, NEG)

    flash(ks_ref, vst_ref, 0, sel_negb(False), sel_negb(True), 1)
    flash(kw_ref, vwt_ref, jnp.maximum(q0 - (NSA_WINDOW - 1), 0) // TK, win_negb, win_negb, 2)

    g_t = jax.nn.sigmoid(gl_ref[...]).T
    for c in range(NSA_HEADS // 2):
        halves = []
        for e in range(2):
            h = 2 * c + e
            g, j = divmod(h, NSA_J)
            halves.append(sum(g_t[3 * h + br:3 * h + br + 1] * res_ref[br, g, :, j * T:(j + 1) * T]
                              for br in range(3)))
        o_ref[:, c * T:(c + 1) * T] = jnp.concatenate(halves, axis=0).T.astype(o_ref.dtype)


def nsa_attention(q, gate_logits, k_sw, vt_sw, kc, vct, sel_map_t, B, S):
    nq = S // NSA_T
    R = kc.shape[1]
    kw = NSA_G * V7X_LANES
    return pl.pallas_call(
        functools.partial(_nsa_kernel, n_blk=S // NSA_SEL_BLOCK),
        grid=(B, nq),
        in_specs=[pl.BlockSpec((NSA_T, MIX), lambda b, i: (b * nq + i, 0)),
                  pl.BlockSpec((NSA_T, V7X_LANES), lambda b, i: (b * nq + i, 0)),
                  pl.BlockSpec((S, kw), lambda b, i: (b, 0)),
                  pl.BlockSpec((V7X_LANES, S), lambda b, i: (0, b)),
                  pl.BlockSpec((S, kw), lambda b, i: (b, 1)),
                  pl.BlockSpec((V7X_LANES, S), lambda b, i: (1, b)),
                  pl.BlockSpec((1, R, kw), lambda b, i: (b, 0, 0)),
                  pl.BlockSpec((1, V7X_LANES, R), lambda b, i: (b, 0, 0)),
                  pl.BlockSpec((V7X_LANES, R), lambda b, i: (0, 0))],
        out_specs=pl.BlockSpec((NSA_T, MIX), lambda b, i: (b * nq + i, 0)),
        out_shape=jax.ShapeDtypeStruct((B * S, MIX), BF16),
        scratch_shapes=[pltpu.VMEM((NSA_G, V7X_LANES, NSA_J * NSA_T), BF16),
                        pltpu.VMEM((NSA_G, V7X_LANES, NSA_J * NSA_T), BF16),
                        pltpu.VMEM((NSA_G, V7X_LANES, NSA_T), F32),
                        pltpu.VMEM((NSA_G, NSA_TK, NSA_J * NSA_T), BF16),
                        pltpu.VMEM((NSA_G, NSA_DH + NSA_LROWS, NSA_J * NSA_T), F32),
                        pltpu.VMEM((3, NSA_G, NSA_DH, NSA_J * NSA_T), F32)],
        compiler_params=_params("parallel", "parallel"),
        name="nsa_attention",
    )(q, gate_logits, k_sw, vt_sw, k_sw, vt_sw, kc, vct, sel_map_t)


def _sel_map_t(S):
    n_c = S // NSA_CMP_STRIDE
    c0 = np.arange(n_c)[None, :] * NSA_CMP_STRIDE
    s0 = np.arange(V7X_LANES)[:, None] * NSA_SEL_BLOCK
    ov = np.clip(np.minimum(c0 + NSA_CMP_LEN, s0 + NSA_SEL_BLOCK) - np.maximum(c0, s0), 0, None)
    return jnp.asarray(ov / NSA_CMP_LEN, dtype=F32)


def _hgrn_kernel(x_ref, lb_ref, ng_ref, o_ref, state_ref):
    C, dk, U = HG_C, HG_DK, HG_SUB

    @pl.when(pl.program_id(1) == 0)
    def _():
        state_ref[...] = jnp.zeros_like(state_ref)

    tri = (lax.broadcasted_iota(jnp.int32, (C, C), 0) >= lax.broadcasted_iota(jnp.int32, (C, C), 1)).astype(F32)
    rows = lax.broadcasted_iota(jnp.int32, (U, 1), 0)
    for h in range(HG_H):
        def col(part):
            return x_ref[:, part * MIX + h * dk: part * MIX + (h + 1) * dk]
        q, z, v, gate = col(0), col(1), col(2), col(3)
        lb = lb_ref[:, h * dk:(h + 1) * dk]
        f = jnp.maximum(lb + (1.0 - lb) * jax.nn.sigmoid(z), HG_FMIN)
        kk = 1.0 - f
        b = jnp.dot(tri, jnp.log(f), preferred_element_type=F32, precision=lax.Precision.HIGHEST)
        st = state_ref[h]
        o = lax.dot_general((q * jnp.exp(b)).astype(BF16), st.astype(BF16), (((1,), (1,)), ((), ())),
                            preferred_element_type=F32)
        v_bf = v.astype(BF16)
        parts = []
        for i in range(C // U):
            r0 = i * U
            b_r, q_r, o_r = b[r0:r0 + U], q[r0:r0 + U], o[r0:r0 + U]
            if i > 0:
                b_e = b[r0 - 1:r0]
                qe = (q_r * jnp.exp(b_r - b_e)).astype(BF16)
                ke = (kk[0:r0] * jnp.exp(b_e - b[0:r0])).astype(BF16)
                a = lax.dot_general(qe, ke, (((1,), (1,)), ((), ())), preferred_element_type=F32)
                o_r = o_r + jnp.dot(a.astype(BF16), v_bf[0:r0], preferred_element_type=F32)
            for s in range(U):
                d = jnp.where(rows >= s, b_r - b[r0 + s:r0 + s + 1], NEG)
                a = jnp.sum(q_r * jnp.exp(d) * kk[r0 + s:r0 + s + 1], axis=-1, keepdims=True)
                o_r = o_r + a * v[r0 + s:r0 + s + 1]
            parts.append(o_r)
        o = jnp.concatenate(parts, axis=0)
        b_last = b[C - 1:C]
        kd = (kk * jnp.exp(b_last - b)).astype(BF16)
        state_ref[h] = st * jnp.exp(b_last) + jnp.dot(v.T.astype(BF16), kd, preferred_element_type=F32)
        y = o * lax.rsqrt(jnp.mean(o * o, axis=-1, keepdims=True) + LN_EPS) * ng_ref[:, h * dk:(h + 1) * dk]
        o_ref[:, h * dk:(h + 1) * dk] = (y * (gate * jax.nn.sigmoid(gate))).astype(o_ref.dtype)


def hgrn2(b_in, lb, norm_g, B, S):
    n = S // HG_C
    return pl.pallas_call(
        _hgrn_kernel,
        grid=(B, n),
        in_specs=[pl.BlockSpec((HG_C, 4 * MIX), lambda b, c: (b * n + c, 0)),
                  pl.BlockSpec((1, MIX), lambda b, c: (0, 0)),
                  pl.BlockSpec((1, MIX), lambda b, c: (0, 0))],
        out_specs=pl.BlockSpec((HG_C, MIX), lambda b, c: (b * n + c, 0)),
        out_shape=jax.ShapeDtypeStruct((B * S, MIX), BF16),
        scratch_shapes=[pltpu.VMEM((HG_H, HG_DK, HG_DK), F32)],
        compiler_params=_params("parallel", "arbitrary"),
        name="hgrn2",
    )(b_in, lb.reshape(1, MIX), norm_g.reshape(1, MIX))


def _pool_kernel(c_ref, halo_ref, w_ref, scale_ref, o_ref, ext_ref, *, tiles_per_seq):
    tm = c_ref.shape[0]
    first = pl.program_id(0) % tiles_per_seq == 0
    ext_ref[0:POOL_HALO, :] = jnp.where(first, 0.0, halo_ref[...])
    ext_ref[POOL_HALO:, :] = c_ref[...]
    t_in_seq = (pl.program_id(0) % tiles_per_seq) * tm + lax.broadcasted_iota(jnp.int32, (tm, 1), 0)
    for gi, win in enumerate(POOL_WINDOWS):
        lo, hi = gi * V7X_LANES, (gi + 1) * V7X_LANES
        tot = ext_ref[POOL_HALO:, lo:hi]
        for d in range(1, win):
            tot = tot + ext_ref[POOL_HALO - d:POOL_HALO - d + tm, lo:hi]
        cnt = jnp.minimum(t_in_seq + 1, win).astype(F32)
        p = tot / cnt - c_ref[:, lo:hi]
        y = jnp.dot(p.astype(BF16), w_ref[gi], preferred_element_type=F32)
        o_ref[:, lo:hi] = (y * scale_ref[:, lo:hi]).astype(o_ref.dtype)


def pool_mixer(c_in, col, w, scale, S, tm=512):
    M = c_in.shape[0]
    return pl.pallas_call(
        functools.partial(_pool_kernel, tiles_per_seq=S // tm),
        grid=(M // tm,),
        in_specs=[pl.BlockSpec((tm, MIX), lambda i: (i, col)),
                  pl.BlockSpec((POOL_HALO, MIX), lambda i: (jnp.maximum(i * (tm // POOL_HALO) - 1, 0), col)),
                  pl.BlockSpec((len(POOL_WINDOWS), V7X_LANES, V7X_LANES), lambda i: (0, 0, 0)),
                  pl.BlockSpec((1, MIX), lambda i: (0, 0))],
        out_specs=pl.BlockSpec((tm, MIX), lambda i: (i, 0)),
        out_shape=jax.ShapeDtypeStruct((M, MIX), BF16),
        scratch_shapes=[pltpu.VMEM((tm + POOL_HALO, MIX), F32)],
        compiler_params=_params("parallel"),
        name="pool_mixer",
    )(c_in, c_in, w, scale.reshape(1, MIX))


def _sgu_kernel(d_ref, g_ref, b_ref, ws_ref, bs_ref, o_ref):
    u = d_ref[:, 0:MIX]
    v = _ln(jax.nn.gelu(d_ref[:, MIX:2 * MIX]), g_ref[...], b_ref[...]).astype(BF16)
    causal = (lax.broadcasted_iota(jnp.int32, (SG_CHUNK, SG_CHUNK), 0)
              >= lax.broadcasted_iota(jnp.int32, (SG_CHUNK, SG_CHUNK), 1))
    hd = MIX // SG_H
    for g in range(SG_H):
        wm = jnp.where(causal, ws_ref[g], 0.0).astype(BF16)
        vs = jnp.dot(wm, v[:, g * hd:(g + 1) * hd], preferred_element_type=F32) + bs_ref[:, g:g + 1]
        o_ref[:, g * hd:(g + 1) * hd] = (jax.nn.gelu(u[:, g * hd:(g + 1) * hd]) * vs).astype(o_ref.dtype)


def sgu_mixer(d_in, col, ln_g, ln_b, ws, bs):
    M = d_in.shape[0]
    return pl.pallas_call(
        _sgu_kernel,
        grid=(M // SG_CHUNK,),
        in_specs=[pl.BlockSpec((SG_CHUNK, 2 * MIX), lambda i: (i, col)),
                  pl.BlockSpec((1, MIX), lambda i: (0, 0)),
                  pl.BlockSpec((1, MIX), lambda i: (0, 0)),
                  pl.BlockSpec((SG_H, SG_CHUNK, SG_CHUNK), lambda i: (0, 0, 0)),
                  pl.BlockSpec((SG_CHUNK, SG_H), lambda i: (0, 0))],
        out_specs=pl.BlockSpec((SG_CHUNK, MIX), lambda i: (i, 0)),
        out_shape=jax.ShapeDtypeStruct((M, MIX), BF16),
        compiler_params=_params("parallel"),
        name="sgu_mixer",
    )(d_in, ln_g.reshape(1, MIX), ln_b.reshape(1, MIX), ws, bs.T)


def _xattn_kernel(q_ref, k_ref, v_ref, o_ref):
    for h in range(X_HEADS):
        sl = slice(h * X_DH, (h + 1) * X_DH)
        s = lax.dot_general(q_ref[:, sl], k_ref[:, sl], (((1,), (1,)), ((), ())),
                            preferred_element_type=F32) * (X_DH ** -0.5)
        p = jnp.exp(s - jnp.max(s, axis=-1, keepdims=True))
        p = p / jnp.sum(p, axis=-1, keepdims=True)
        o_ref[:, sl] = jnp.dot(p.astype(BF16), v_ref[:, sl], preferred_element_type=F32).astype(o_ref.dtype)


def cross_attention(q, k, v, S, mem_len, tm=512):
    M, W = q.shape
    return pl.pallas_call(
        _xattn_kernel,
        grid=(M // tm,),
        in_specs=[pl.BlockSpec((tm, W), lambda i: (i, 0)),
                  pl.BlockSpec((mem_len, W), lambda i: (i * tm // S, 0)),
                  pl.BlockSpec((mem_len, W), lambda i: (i * tm // S, 0))],
        out_specs=pl.BlockSpec((tm, W), lambda i: (i, 0)),
        out_shape=jax.ShapeDtypeStruct((M, W), BF16),
        compiler_params=_params("parallel"),
        name="cross_attention",
    )(q, k, v)


def _ffn_in_kernel(x_ref, halo_ref, wg32_ref, wu32_ref, cw_ref, cb_ref, o_ref, wg_ref, wu_ref, *, tiles_per_seq):
    @pl.when(pl.program_id(1) == 0)
    def _():
        wg_ref[...] = wg32_ref[...].astype(BF16)
        wu_ref[...] = wu32_ref[...].astype(BF16)

    first = pl.program_id(1) % tiles_per_seq == 0
    x = x_ref[...]
    g_main = jnp.dot(x, wg_ref[...], preferred_element_type=F32)
    g_halo = jnp.dot(halo_ref[...], wg_ref[...], preferred_element_type=F32)
    g_halo = jnp.where(first, 0.0, g_halo)
    ext = jnp.concatenate([g_halo, g_main], axis=0)
    n = ext.shape[0]
    conv = cb_ref[...] + cw_ref[CONV_W - 1:CONV_W] * g_main
    for d in range(1, CONV_W):
        conv = conv + cw_ref[CONV_W - 1 - d:CONV_W - d] * pltpu.roll(ext, d, axis=0)[CONV_HALO:n]
    up = jnp.dot(x, wu_ref[...], preferred_element_type=F32)
    o_ref[...] = (conv * jax.nn.sigmoid(conv) * up).astype(o_ref.dtype)


def ffn_in(xbf, w_in_all, conv_w_all, conv_b_all, layer, S, tm=1024, tn=512):
    M, D = xbf.shape
    F = w_in_all.shape[2] // 2
    nj = F // tn
    return pl.pallas_call(
        functools.partial(_ffn_in_kernel, tiles_per_seq=S // tm),
        grid=(nj, M // tm),
        in_specs=[pl.BlockSpec((tm, D), lambda j, i: (i, 0)),
                  pl.BlockSpec((CONV_HALO, D), lambda j, i: (jnp.maximum(i * (tm // CONV_HALO) - 1, 0), 0)),
                  pl.BlockSpec((None, D, tn), lambda j, i: (layer, 0, j)),
                  pl.BlockSpec((None, D, tn), lambda j, i: (layer, 0, j + nj)),
                  pl.BlockSpec((None, CONV_W, tn), lambda j, i: (layer, 0, j)),
                  pl.BlockSpec((None, 1, tn), lambda j, i: (layer, 0, j))],
        out_specs=pl.BlockSpec((tm, tn), lambda j, i: (i, j)),
        out_shape=jax.ShapeDtypeStruct((M, F), BF16),
        scratch_shapes=[pltpu.VMEM((D, tn), BF16), pltpu.VMEM((D, tn), BF16)],
        compiler_params=_params("parallel", "arbitrary"),
        name="ffn_in",
    )(xbf, xbf, w_in_all, w_in_all, conv_w_all, conv_b_all.reshape(conv_b_all.shape[0], 1, F))


def _nsa_weights(w_q, w_kv, w_gate):
    D = w_q.shape[0]
    gap = jnp.zeros((D, V7X_LANES - NSA_DH), F32)
    per_branch = 2 * NSA_G * NSA_DH
    k_cols, v_cols = [], []
    for br in (1, 2):
        base = br * per_branch
        for g in range(NSA_G):
            k_cols += [w_kv[:, base + g * NSA_DH:base + (g + 1) * NSA_DH], gap]
        v_cols.append(w_kv[:, base + NSA_G * NSA_DH:base + per_branch])
    gate = jnp.pad(w_gate, ((0, 0), (0, V7X_LANES - w_gate.shape[1])))
    return jnp.concatenate([w_q, gate, w_kv[:, :per_branch]] + k_cols + v_cols, axis=1).astype(BF16)


def _compress_weights(cmp_pos, cmp_w):
    half = NSA_CMP_LEN // 2
    eye_g = jnp.eye(NSA_G, dtype=F32)
    ws, ps = [], []
    for part in range(2):
        blocks = []
        for kv in range(2):
            w = cmp_w[kv, part * half:(part + 1) * half]
            full = jnp.einsum('lde,gh->lgdhe', w, eye_g)
            z = jnp.zeros_like(full)
            pair = [full, z] if kv == 0 else [z, full]
            blocks.append(jnp.stack(pair, axis=3))
        w_all = jnp.stack(blocks, axis=1)
        w_all = w_all.reshape(half * 2 * NSA_G * NSA_DH, 2 * NSA_G * NSA_DH)
        gap = jnp.zeros((w_all.shape[0], V7X_LANES - NSA_DH), F32)
        w_all = jnp.concatenate([w_all[:, :NSA_DH], gap, w_all[:, NSA_DH:2 * NSA_DH], gap,
                                 w_all[:, 2 * NSA_DH:]], axis=1)
        ws.append(w_all.astype(BF16))
        p = cmp_pos[:, part * half:(part + 1) * half]
        p = jnp.broadcast_to(p.transpose(1, 0, 2)[:, :, None, :], (half, 2, NSA_G, NSA_DH))
        ps.append(p.reshape(-1))
    return jnp.stack(ps, axis=0), ws[0], ws[1]


def kernel(x, mem, w_in, nsa_cmp_pos, nsa_cmp_w, hgrn_lb_logits, hgrn_norm_g, pool_w, pool_scale, sg_ln_g, sg_ln_b, sg_w, sg_b, w_branch, w_mix_out, ln_mix_g, ln_mix_b, mem_ln_g, mem_ln_b, xattn_q, xattn_k, xattn_v, xattn_o, ln_x_g, ln_x_b, ffn_in_w, ffn_conv_w, ffn_conv_b, ffn_out, ln_ffn_g, ln_ffn_b):
    B, S, D = x.shape
    M = B * S
    mem_len = mem.shape[1]
    assert D == D_MODEL and S % 512 == 0 and S // NSA_SEL_BLOCK <= V7X_LANES

    sm = jax.nn.softmax(hgrn_lb_logits.astype(F32), axis=0)
    lower_bounds = jnp.cumsum(sm, axis=0) - sm[0]
    sel_map_t = _sel_map_t(S)
    mem_n = layer_norm_rows(mem.reshape(B * mem_len, D), mem_ln_g, mem_ln_b, BF16)

    xf = x.reshape(M, D)
    xbf = xf.astype(BF16)
    o0, o1, o2, o3, o4, o5 = np.cumsum([MIX, 768, 24, 4 * MIX, MIX, 2 * MIX])
    for l in range(DEPTH):
        w = w_in[l]
        w_nsa = _nsa_weights(w[:, :o0], w[:, o0:o1], w[:, o1:o2])
        w_bdc = jnp.concatenate([w[:, o2:o3], w[:, o4:o5], w[:, o3:o4]], axis=1).astype(BF16)
        w_merge = w[:, o5:].astype(BF16)

        a_q, a_g, kv_cmp, k_sw, vt_sw = nsa_project(xbf, w_nsa, S)
        bdc = matmul(xbf, w_bdc, F32)

        pos2, wlo, whi = _compress_weights(nsa_cmp_pos[l], nsa_cmp_w[l])
        kc, vct = nsa_compress(kv_cmp, pos2, wlo, whi, B, S)
        o_a = nsa_attention(a_q, a_g, k_sw, vt_sw, kc, vct, sel_map_t, B, S)
        o_b = hgrn2(bdc, lower_bounds[l], hgrn_norm_g[l], B, S)
        o_c = pool_mixer(bdc, 6, pool_w[l].astype(BF16), pool_scale[l], S)
        o_d = sgu_mixer(bdc, 2, sg_ln_g[l], sg_ln_b[l], sg_w[l], sg_b[l])

        gated = merge_branches(xbf, (o_a, o_b, o_c, o_d), w_merge, w_branch[l].astype(BF16))
        xf, xbf = matmul_residual_ln(gated, w_mix_out[l].astype(BF16), xf, ln_mix_g[l], ln_mix_b[l])

        xq = matmul(xbf, xattn_q[l].astype(BF16), BF16)
        xk = matmul(mem_n, xattn_k[l].astype(BF16), BF16)
        xv = matmul(mem_n, xattn_v[l].astype(BF16), BF16)
        xo = cross_attention(xq, xk, xv, S, mem_len)
        xf, xbf = matmul_residual_ln(xo, xattn_o[l].astype(BF16), xf, ln_x_g[l], ln_x_b[l])

        act = ffn_in(xbf, ffn_in_w, ffn_conv_w, ffn_conv_b, l, S)
        xf, xbf = matmul_residual_ln(act, ffn_out[l].astype(BF16), xf, ln_ffn_g[l], ln_ffn_b[l])
    return xf.reshape(B, S, D)
```

```python
import functools

import numpy as np
import jax
import jax.numpy as jnp
from jax import lax
from jax.experimental import pallas as pl
from jax.experimental.pallas import tpu as pltpu

F32 = jnp.float32
BF16 = jnp.bfloat16

V7X_LANES = 128
V7X_VMEM_BYTES = 64 * 1024 * 1024
VMEM_LIMIT = 48 * 1024 * 1024

D_MODEL = 2048
DEPTH = 2
MIX = D_MODEL // 4
N_BRANCH = 4

NSA_DH = 64
NSA_G = 2
NSA_J = 4
NSA_HEADS = NSA_G * NSA_J
NSA_CMP_LEN = 32
NSA_CMP_STRIDE = 16
NSA_SEL_BLOCK = 64
NSA_N_SEL = 16
NSA_WINDOW = 512
NSA_T = 128
NSA_TK = 512
NSA_LROWS = 16
NSA_FORCE = 1e6

HG_H = 4
HG_DK = 128
HG_C = 64
HG_SUB = 16
HG_FMIN = 1e-6

POOL_WINDOWS = (2, 4, 8, 16)
POOL_HALO = 16

SG_CHUNK = 128
SG_H = 4

X_HEADS = 4
X_DH = 128

D_FF = 5632
CONV_W = 3
CONV_HALO = 16

LN_EPS = 1e-5
NEG = -1e30
ALPHA = (2 * DEPTH) ** 0.25

ALIBI = [float(2.0 ** (-8.0 * (h + 1) / NSA_HEADS)) for h in range(NSA_HEADS)]


def _params(*sem):
    return pltpu.CompilerParams(dimension_semantics=sem, vmem_limit_bytes=VMEM_LIMIT)


def _ln(y, g, b):
    mu = jnp.mean(y, axis=-1, keepdims=True)
    d = y - mu
    var = jnp.mean(d * d, axis=-1, keepdims=True)
    return d * lax.rsqrt(var + LN_EPS) * g + b


def _mm_kernel(a_ref, w_ref, o_ref):
    o_ref[...] = jnp.dot(a_ref[...], w_ref[...], preferred_element_type=F32).astype(o_ref.dtype)


def matmul(a, w, out_dtype, tm=512):
    M, K = a.shape
    N = w.shape[1]
    tm = min(tm, M)
    tn = min(N, 512)
    return pl.pallas_call(
        _mm_kernel,
        grid=(N // tn, M // tm),
        in_specs=[pl.BlockSpec((tm, K), lambda j, i: (i, 0)),
                  pl.BlockSpec((K, tn), lambda j, i: (0, j))],
        out_specs=pl.BlockSpec((tm, tn), lambda j, i: (i, j)),
        out_shape=jax.ShapeDtypeStruct((M, N), out_dtype),
        compiler_params=_params("parallel", "parallel"),
        name="matmul",
    )(a, w)


def _mm_resident_kernel(a_ref, w_ref, o_ref, *, tn):
    a = a_ref[...]
    for n0 in range(0, w_ref.shape[1], tn):
        o_ref[:, n0:n0 + tn] = jnp.dot(a, w_ref[:, n0:n0 + tn], preferred_element_type=F32).astype(o_ref.dtype)


def matmul_resident(a, w, out_dtype, tm=256, tn=512):
    M, K = a.shape
    N = w.shape[1]
    return pl.pallas_call(
        functools.partial(_mm_resident_kernel, tn=tn),
        grid=(M // tm,),
        in_specs=[pl.BlockSpec((tm, K), lambda i: (i, 0)),
                  pl.BlockSpec((K, N), lambda i: (0, 0))],
        out_specs=pl.BlockSpec((tm, N), lambda i: (i, 0)),
        out_shape=jax.ShapeDtypeStruct((M, N), out_dtype),
        compiler_params=_params("parallel"),
        name="matmul_resident",
    )(a, w)


def _mm_ln_kernel(a_ref, w_ref, x_ref, g_ref, b_ref, o_ref, obf_ref, *scratch, n_k):
    part = jnp.dot(a_ref[...], w_ref[...], preferred_element_type=F32)

    def finish(total):
        y = _ln(ALPHA * x_ref[...] + total, g_ref[...], b_ref[...])
        o_ref[...] = y
        obf_ref[...] = y.astype(BF16)

    if n_k == 1:
        finish(part)
        return
    acc_ref, = scratch
    k = pl.program_id(1)

    @pl.when(k == 0)
    def _():
        acc_ref[...] = part

    @pl.when((k > 0) & (k < n_k - 1))
    def _():
        acc_ref[...] += part

    @pl.when(k == n_k - 1)
    def _():
        finish(acc_ref[...] + part)


MM_LN_WEIGHT_TILE_BYTES = 16 * 1024 * 1024


def matmul_residual_ln(a, w, x, g, b, tm=512):
    M, K = a.shape
    D = w.shape[1]
    n_k = next(n for n in range(1, K) if K % n == 0 and (K // n) % V7X_LANES == 0
               and (K // n) * D * 2 * 2 <= MM_LN_WEIGHT_TILE_BYTES)
    tk = K // n_k
    return pl.pallas_call(
        functools.partial(_mm_ln_kernel, n_k=n_k),
        grid=(M // tm, n_k),
        in_specs=[pl.BlockSpec((tm, tk), lambda i, k: (i, k)),
                  pl.BlockSpec((tk, D), lambda i, k: (k, 0)),
                  pl.BlockSpec((tm, D), lambda i, k: (i, 0)),
                  pl.BlockSpec((1, D), lambda i, k: (0, 0)),
                  pl.BlockSpec((1, D), lambda i, k: (0, 0))],
        out_specs=[pl.BlockSpec((tm, D), lambda i, k: (i, 0)),
                   pl.BlockSpec((tm, D), lambda i, k: (i, 0))],
        out_shape=[jax.ShapeDtypeStruct((M, D), F32), jax.ShapeDtypeStruct((M, D), BF16)],
        scratch_shapes=[pltpu.VMEM((tm, D), F32)] if n_k > 1 else [],
        compiler_params=_params("parallel", "arbitrary"),
        name="matmul_residual_ln",
    )(a, w, x, g.reshape(1, D), b.reshape(1, D))


def _ln_kernel(x_ref, g_ref, b_ref, o_ref):
    o_ref[...] = _ln(x_ref[...], g_ref[...], b_ref[...]).astype(o_ref.dtype)


def layer_norm_rows(x, g, b, out_dtype, tm=256):
    M, D = x.shape
    return pl.pallas_call(
        _ln_kernel,
        grid=(M // tm,),
        in_specs=[pl.BlockSpec((tm, D), lambda i: (i, 0)),
                  pl.BlockSpec((1, D), lambda i: (0, 0)),
                  pl.BlockSpec((1, D), lambda i: (0, 0))],
        out_specs=pl.BlockSpec((tm, D), lambda i: (i, 0)),
        out_shape=jax.ShapeDtypeStruct((M, D), out_dtype),
        compiler_params=_params("parallel"),
        name="layer_norm_rows",
    )(x, g.reshape(1, D), b.reshape(1, D))


def _merge_kernel(x_ref, *refs):
    o_refs, wm_refs, (wb_ref, o_ref) = refs[:N_BRANCH], refs[N_BRANCH:2 * N_BRANCH], refs[2 * N_BRANCH:]
    x = x_ref[...]
    acc = None
    for n in range(N_BRANCH):
        gate = jax.nn.sigmoid(jnp.dot(x, wm_refs[n][...], preferred_element_type=F32))
        y = jnp.dot(o_refs[n][...], wb_ref[n], preferred_element_type=F32)
        acc = gate * y if acc is None else acc + gate * y
    o_ref[...] = acc.astype(o_ref.dtype)


def merge_branches(xbf, branches, w_merge, w_branch, tm=512, tn=512):
    M, D = xbf.shape
    W = branches[0].shape[1]
    return pl.pallas_call(
        _merge_kernel,
        grid=(D // tn, M // tm),
        in_specs=[pl.BlockSpec((tm, D), lambda j, i: (i, 0))]
        + [pl.BlockSpec((tm, W), lambda j, i: (i, 0))] * N_BRANCH
        + [pl.BlockSpec((D, tn), lambda j, i, n=n: (0, n * (D // tn) + j)) for n in range(N_BRANCH)]
        + [pl.BlockSpec((N_BRANCH, W, tn), lambda j, i: (0, 0, j))],
        out_specs=pl.BlockSpec((tm, tn), lambda j, i: (i, j)),
        out_shape=jax.ShapeDtypeStruct((M, D), BF16),
        compiler_params=_params("parallel", "parallel"),
        name="merge_branches",
    )(xbf, *branches, *([w_merge] * N_BRANCH), w_branch)


def _aux_lanes(k, lane, vals):
    for i, v in enumerate(vals):
        k = jnp.where(lane == NSA_DH + i, v, k)
    return k


def _compress_kernel(rows_ref, pos_ref, wlo_ref, whi_ref, kc_ref, vct_ref):
    rows = rows_ref[0]
    lo = jnp.dot((rows + pos_ref[0:1]).astype(BF16), wlo_ref[...], preferred_element_type=F32)
    hi = jnp.dot((rows + pos_ref[1:2]).astype(BF16), whi_ref[...], preferred_element_type=F32)
    n = rows.shape[0]
    res = lo + pltpu.roll(hi, n - 1, axis=0)
    kw = NSA_G * V7X_LANES
    c = lax.broadcasted_iota(jnp.int32, (n, kw), 0)
    lane = lax.broadcasted_iota(jnp.int32, (n, kw), 1) % V7X_LANES
    kc = _aux_lanes(res[:, :kw], lane, [(c // 8).astype(F32), (c % 8).astype(F32), 1.0, 1.0])
    kc_ref[0] = kc.astype(kc_ref.dtype)
    vct_ref[0] = res[:, kw:].T.astype(vct_ref.dtype)


def nsa_compress(kv_cmp, pos2, wlo, whi, B, S):
    R = S // NSA_CMP_STRIDE
    C = kv_cmp.shape[1]
    CO = wlo.shape[1]
    rows = kv_cmp.reshape(B, R, NSA_CMP_STRIDE * C)
    return pl.pallas_call(
        _compress_kernel,
        grid=(B,),
        in_specs=[pl.BlockSpec((1, R, NSA_CMP_STRIDE * C), lambda b: (b, 0, 0)),
                  pl.BlockSpec((2, NSA_CMP_STRIDE * C), lambda b: (0, 0)),
                  pl.BlockSpec((NSA_CMP_STRIDE * C, CO), lambda b: (0, 0)),
                  pl.BlockSpec((NSA_CMP_STRIDE * C, CO), lambda b: (0, 0))],
        out_specs=[pl.BlockSpec((1, R, NSA_G * V7X_LANES), lambda b: (b, 0, 0)),
                   pl.BlockSpec((1, V7X_LANES, R), lambda b: (b, 0, 0))],
        out_shape=[jax.ShapeDtypeStruct((B, R, NSA_G * V7X_LANES), BF16),
                   jax.ShapeDtypeStruct((B, V7X_LANES, R), BF16)],
        compiler_params=_params("parallel"),
        name="nsa_compress",
    )(rows, pos2, wlo, whi)


def _nsa_proj_kernel(x_ref, w_ref, q_ref, gl_ref, kvc_ref, k_ref, vt_ref, *, tiles_per_seq):
    tm = x_ref.shape[0]
    h = jnp.dot(x_ref[...], w_ref[...], preferred_element_type=F32)
    c0 = MIX
    c1 = c0 + V7X_LANES
    c2 = c1 + 2 * NSA_G * NSA_DH
    c3 = c2 + 2 * NSA_G * V7X_LANES
    q_ref[...] = h[:, :c0].astype(q_ref.dtype)
    gl_ref[...] = h[:, c0:c1]
    kvc_ref[...] = h[:, c1:c2]
    pos = (pl.program_id(0) % tiles_per_seq) * tm + lax.broadcasted_iota(jnp.int32, (tm, c3 - c2), 0)
    lane = lax.broadcasted_iota(jnp.int32, (tm, c3 - c2), 1) % V7X_LANES
    k = _aux_lanes(h[:, c2:c3], lane, [(pos // NSA_T).astype(F32), (pos % NSA_T).astype(F32), 1.0])
    k_ref[...] = k.astype(k_ref.dtype)
    vt_ref[...] = h[:, c3:].T.astype(vt_ref.dtype)


def nsa_project(xbf, w_all, S, tm=512):
    M, D = xbf.shape
    N = w_all.shape[1]
    kw = 2 * NSA_G * V7X_LANES
    return pl.pallas_call(
        functools.partial(_nsa_proj_kernel, tiles_per_seq=S // tm),
        grid=(M // tm,),
        in_specs=[pl.BlockSpec((tm, D), lambda i: (i, 0)),
                  pl.BlockSpec((D, N), lambda i: (0, 0))],
        out_specs=[pl.BlockSpec((tm, MIX), lambda i: (i, 0)),
                   pl.BlockSpec((tm, V7X_LANES), lambda i: (i, 0)),
                   pl.BlockSpec((tm, 2 * NSA_G * NSA_DH), lambda i: (i, 0)),
                   pl.BlockSpec((tm, kw), lambda i: (i, 0)),
                   pl.BlockSpec((2 * V7X_LANES, tm), lambda i: (0, i))],
        out_shape=[jax.ShapeDtypeStruct((M, MIX), BF16),
                   jax.ShapeDtypeStruct((M, V7X_LANES), F32),
                   jax.ShapeDtypeStruct((M, 2 * NSA_G * NSA_DH), F32),
                   jax.ShapeDtypeStruct((M, kw), BF16),
                   jax.ShapeDtypeStruct((2 * V7X_LANES, M), BF16)],
        compiler_params=_params("parallel"),
        name="nsa_project",
    )(xbf, w_all)


def _nsa_kernel(q_ref, gl_ref, ks_ref, vst_ref, kw_ref, vwt_ref, kc_ref, vct_ref, selmap_ref, o_ref,
                qt_ref, qtc_ref, selneg_ref, p_ref, acc_ref, res_ref, *, n_blk):
    T, TK, DH = NSA_T, NSA_TK, NSA_DH
    qb = pl.program_id(1)
    q0 = qb * T
    t_row = q0 + lax.broadcasted_iota(jnp.int32, (1, T), 1)
    aux_row = lax.broadcasted_iota(jnp.int32, (DH, T), 0)

    qf = q_ref[...].astype(F32) * (DH ** -0.5)
    q_t = [qf[:, c * T:(c + 1) * T].T for c in range(NSA_HEADS // 2)]
    for g in range(NSA_G):
        main, cmp = [], []
        for j in range(NSA_J):
            h = g * NSA_J + j
            sl = ALIBI[h]
            off = (-sl * T) * qb.astype(F32)
            aux = jnp.where(aux_row == 0, sl * T, jnp.where(aux_row == 2, off, 0.0))
            aux_main = jnp.where(aux_row == 1, sl, aux)
            aux_cmp = jnp.where(aux_row == 1, sl * NSA_CMP_STRIDE,
                                jnp.where(aux_row == 3, sl * (NSA_CMP_LEN - 1), aux))
            qh = q_t[h // 2][(h % 2) * DH:(h % 2 + 1) * DH]
            main.append(jnp.concatenate([qh, aux_main], axis=0))
            cmp.append(jnp.concatenate([qh, aux_cmp], axis=0))
        qt_ref[g] = jnp.concatenate(main, axis=1).astype(BF16)
        qtc_ref[g] = jnp.concatenate(cmp, axis=1).astype(BF16)

    def per_head(x):
        return jnp.concatenate([x] * NSA_J, axis=1)

    n_c = kc_ref.shape[1]
    c_col = lax.broadcasted_iota(jnp.int32, (n_c, T), 0)
    negb_c = per_head(jnp.where(c_col * NSA_CMP_STRIDE + (NSA_CMP_LEN - 1) <= t_row, 0.0, NEG))
    has_key = per_head(t_row >= NSA_CMP_LEN - 1)
    blk = lax.broadcasted_iota(jnp.int32, (V7X_LANES, T), 0)
    blk_f = blk.astype(F32)
    cur = t_row // NSA_SEL_BLOCK
    forced = (blk == 0) | (blk == cur) | (blk == cur - 1)
    future = blk * NSA_SEL_BLOCK > t_row
    for g in range(NSA_G):
        kc = kc_ref[0, :, g * V7X_LANES:(g + 1) * V7X_LANES]
        vct = vct_ref[0, g * DH:(g + 1) * DH, :]
        s = jnp.dot(kc, qtc_ref[g], preferred_element_type=F32) + negb_c
        p = jnp.exp(s - jnp.max(s, axis=0, keepdims=True))
        l = jnp.sum(p, axis=0, keepdims=True)
        p = p * jnp.where(has_key, 1.0 / l, 0.0)
        res_ref[0, g] = jnp.dot(vct, p.astype(BF16), preferred_element_type=F32)
        psum = sum(p[:, j * T:(j + 1) * T] for j in range(NSA_J))
        imp = jnp.dot(selmap_ref[...], psum, preferred_element_type=F32, precision=lax.Precision.HIGHEST)
        work = jnp.where(forced, NSA_FORCE, jnp.where(future, NEG, imp))
        if n_blk < V7X_LANES:
            work = jnp.where(blk >= n_blk, -jnp.inf, work)
        selneg = jnp.full((V7X_LANES, T), NEG, F32)
        for _ in range(min(NSA_N_SEL, n_blk)):
            mx = jnp.max(work, axis=0, keepdims=True)
            first = jnp.min(jnp.where(work == mx, blk_f, float(V7X_LANES)), axis=0, keepdims=True)
            hit = blk_f == first
            selneg = jnp.where(hit, 0.0, selneg)
            work = jnp.where(hit, -jnp.inf, work)
        selneg_ref[g] = selneg

    key_in_tile = lax.broadcasted_iota(jnp.int32, (TK, T), 0)
    ones_rows = jnp.ones((NSA_LROWS, TK), BF16)
    last = qb // (TK // T)

    def flash(k_ref, vt_ref, lo, negb_loop, negb_last, branch):
        def scores(kt, negb_fn):
            k0 = pl.multiple_of(kt * TK, TK)
            return [jnp.dot(k_ref[pl.ds(k0, TK), g * V7X_LANES:(g + 1) * V7X_LANES], qt_ref[g],
                            preferred_element_type=F32) + per_head(negb_fn(g, kt, k0)) for g in range(NSA_G)]

        def fold_values(kt, a):
            k0 = pl.multiple_of(kt * TK, TK)
            for g in range(NSA_G):
                vt = jnp.concatenate([vt_ref[g * DH:(g + 1) * DH, pl.ds(k0, TK)], ones_rows], axis=0)
                acc_ref[g] = a[g] * acc_ref[g] + jnp.dot(vt, p_ref[g], preferred_element_type=F32)

        def softmax(s, m_old):
            m_new, a = [], []
            for g in range(NSA_G):
                m_new.append(jnp.maximum(m_old[g], jnp.max(s[g], axis=0, keepdims=True)))
                p_ref[g] = jnp.exp((s[g] - m_new[g]).astype(BF16))
                a.append(jnp.exp(m_old[g] - m_new[g]))
            return tuple(m_new), tuple(a)

        def step(kt, carry, negb_fn):
            m, a = carry
            s = scores(kt, negb_fn)
            fold_values(kt - 1, a)
            return softmax(s, m)

        acc_ref[...] = jnp.zeros(acc_ref.shape, F32)
        m_init = tuple(jnp.full((1, NSA_J * T), NEG, F32) for _ in range(NSA_G))
        carry = softmax(scores(lo, negb_last), m_init)
        carry = lax.fori_loop(lo + 1, last, lambda kt, c: step(kt, c, negb_loop), carry)
        carry = lax.cond(last > lo, lambda c: step(last, c, negb_last), lambda c: c, carry)
        fold_values(last, carry[1])
        for g in range(NSA_G):
            res_ref[branch, g] = acc_ref[g, :DH] / acc_ref[g, DH:DH + 1]

    def sel_negb(causal):
        def fn(g, kt, k0):
            per_tile = TK // NSA_SEL_BLOCK
            nb = jnp.concatenate(
                [jnp.broadcast_to(selneg_ref[g, pl.ds(kt * per_tile + i, 1), :], (NSA_SEL_BLOCK, T))
                 for i in range(per_tile)], axis=0)
            if causal:
                nb = jnp.where(k0 + key_in_tile <= t_row, nb, NEG)
            return nb
        return fn

    def win_negb(g, kt, k0):
        dist = t_row - (k0 + key_in_tile)
        return jnp.where((dist >= 0) & (dist < NSA_WINDOW), 0.0, NEG)

    flash(ks_ref, vst_ref, 0, sel_negb(False), sel_negb(True), 1)
    flash(kw_ref, vwt_ref, jnp.maximum(q0 - (NSA_WINDOW - 1), 0) // TK, win_negb, win_negb, 2)

    g_t = jax.nn.sigmoid(gl_ref[...]).T
    for c in range(NSA_HEADS // 2):
        halves = []
        for e in range(2):
            h = 2 * c + e
            g, j = divmod(h, NSA_J)
            halves.append(sum(g_t[3 * h + br:3 * h + br + 1] * res_ref[br, g, :, j * T:(j + 1) * T]
                              for br in range(3)))
        o_ref[:, c * T:(c + 1) * T] = jnp.concatenate(halves, axis=0).T.astype(o_ref.dtype)


def nsa_attention(q, gate_logits, k_sw, vt_sw, kc, vct, sel_map_t, B, S):
    nq = S // NSA_T
    R = kc.shape[1]
    kw = NSA_G * V7X_LANES
    return pl.pallas_call(
        functools.partial(_nsa_kernel, n_blk=S // NSA_SEL_BLOCK),
        grid=(B, nq),
        in_specs=[pl.BlockSpec((NSA_T, MIX), lambda b, i: (b * nq + i, 0)),
                  pl.BlockSpec((NSA_T, V7X_LANES), lambda b, i: (b * nq + i, 0)),
                  pl.BlockSpec((S, kw), lambda b, i: (b, 0)),
                  pl.BlockSpec((V7X_LANES, S), lambda b, i: (0, b)),
                  pl.BlockSpec((S, kw), lambda b, i: (b, 1)),
                  pl.BlockSpec((V7X_LANES, S), lambda b, i: (1, b)),
                  pl.BlockSpec((1, R, kw), lambda b, i: (b, 0, 0)),
                  pl.BlockSpec((1, V7X_LANES, R), lambda b, i: (b, 0, 0)),
                  pl.BlockSpec((V7X_LANES, R), lambda b, i: (0, 0))],
        out_specs=pl.BlockSpec((NSA_T, MIX), lambda b, i: (b * nq + i, 0)),
        out_shape=jax.ShapeDtypeStruct((B * S, MIX), BF16),
        scratch_shapes=[pltpu.VMEM((NSA_G, V7X_LANES, NSA_J * NSA_T), BF16),
                        pltpu.VMEM((NSA_G, V7X_LANES, NSA_J * NSA_T), BF16),
                        pltpu.VMEM((NSA_G, V7X_LANES, NSA_T), F32),
                        pltpu.VMEM((NSA_G, NSA_TK, NSA_J * NSA_T), BF16),
                        pltpu.VMEM((NSA_G, NSA_DH + NSA_LROWS, NSA_J * NSA_T), F32),
                        pltpu.VMEM((3, NSA_G, NSA_DH, NSA_J * NSA_T), F32)],
        compiler_params=_params("parallel", "parallel"),
        name="nsa_attention",
    )(q, gate_logits, k_sw, vt_sw, k_sw, vt_sw, kc, vct, sel_map_t)


def _sel_map_t(S):
    n_c = S // NSA_CMP_STRIDE
    c0 = np.arange(n_c)[None, :] * NSA_CMP_STRIDE
    s0 = np.arange(V7X_LANES)[:, None] * NSA_SEL_BLOCK
    ov = np.clip(np.minimum(c0 + NSA_CMP_LEN, s0 + NSA_SEL_BLOCK) - np.maximum(c0, s0), 0, None)
    return jnp.asarray(ov / NSA_CMP_LEN, dtype=F32)


def _hgrn_kernel(x_ref, lb_ref, ng_ref, o_ref, state_ref):
    C, dk, U = HG_C, HG_DK, HG_SUB

    @pl.when(pl.program_id(0) == 0)
    def _():
        state_ref[...] = jnp.zeros_like(state_ref)

    tri = (lax.broadcasted_iota(jnp.int32, (C, C), 0) >= lax.broadcasted_iota(jnp.int32, (C, C), 1)).astype(F32)
    rows = lax.broadcasted_iota(jnp.int32, (U, 1), 0)
    for bh in range(x_ref.shape[0] * HG_H):
        bi, h = divmod(bh, HG_H)

        def col(part):
            return x_ref[bi, :, part * MIX + h * dk: part * MIX + (h + 1) * dk]
        q, z, v, gate = col(0), col(1), col(2), col(3)
        lb = lb_ref[:, h * dk:(h + 1) * dk]
        f = jnp.maximum(lb + (1.0 - lb) * jax.nn.sigmoid(z), HG_FMIN)
        kk = 1.0 - f
        b = jnp.dot(tri, jnp.log(f), preferred_element_type=F32, precision=lax.Precision.HIGHEST)
        st = state_ref[bh]
        o = lax.dot_general((q * jnp.exp(b)).astype(BF16), st.astype(BF16), (((1,), (1,)), ((), ())),
                            preferred_element_type=F32)
        v_bf = v.astype(BF16)
        parts = []
        for i in range(C // U):
            r0 = i * U
            b_r, q_r, o_r = b[r0:r0 + U], q[r0:r0 + U], o[r0:r0 + U]
            if i > 0:
                b_e = b[r0 - 1:r0]
                qe = (q_r * jnp.exp(b_r - b_e)).astype(BF16)
                ke = (kk[0:r0] * jnp.exp(b_e - b[0:r0])).astype(BF16)
                a = lax.dot_general(qe, ke, (((1,), (1,)), ((), ())), preferred_element_type=F32)
                o_r = o_r + jnp.dot(a.astype(BF16), v_bf[0:r0], preferred_element_type=F32)
            for s in range(U):
                d = jnp.where(rows >= s, b_r - b[r0 + s:r0 + s + 1], NEG)
                a = jnp.sum(q_r * jnp.exp(d) * kk[r0 + s:r0 + s + 1], axis=-1, keepdims=True)
                o_r = o_r + a * v[r0 + s:r0 + s + 1]
            parts.append(o_r)
        o = jnp.concatenate(parts, axis=0)
        b_last = b[C - 1:C]
        kd = (kk * jnp.exp(b_last - b)).astype(BF16)
        state_ref[bh] = st * jnp.exp(b_last) + jnp.dot(v.T.astype(BF16), kd, preferred_element_type=F32)
        y = o * lax.rsqrt(jnp.mean(o * o, axis=-1, keepdims=True) + LN_EPS) * ng_ref[:, h * dk:(h + 1) * dk]
        o_ref[bi, :, h * dk:(h + 1) * dk] = (y * (gate * jax.nn.sigmoid(gate))).astype(o_ref.dtype)


def hgrn2(b_in, lb, norm_g, B, S):
    n = S // HG_C
    width = b_in.shape[1]
    out = pl.pallas_call(
        _hgrn_kernel,
        grid=(n,),
        in_specs=[pl.BlockSpec((B, HG_C, 4 * MIX), lambda c: (0, c, 0)),
                  pl.BlockSpec((1, MIX), lambda c: (0, 0)),
                  pl.BlockSpec((1, MIX), lambda c: (0, 0))],
        out_specs=pl.BlockSpec((B, HG_C, MIX), lambda c: (0, c, 0)),
        out_shape=jax.ShapeDtypeStruct((B, S, MIX), BF16),
        scratch_shapes=[pltpu.VMEM((B * HG_H, HG_DK, HG_DK), F32)],
        compiler_params=_params("arbitrary"),
        name="hgrn2",
    )(b_in.reshape(B, S, width), lb.reshape(1, MIX), norm_g.reshape(1, MIX))
    return out.reshape(B * S, MIX)


def _pool_kernel(c_ref, halo_ref, w_ref, scale_ref, o_ref, ext_ref, *, tiles_per_seq):
    tm = c_ref.shape[0]
    first = pl.program_id(0) % tiles_per_seq == 0
    ext_ref[0:POOL_HALO, :] = jnp.where(first, 0.0, halo_ref[...])
    ext_ref[POOL_HALO:, :] = c_ref[...]
    t_in_seq = (pl.program_id(0) % tiles_per_seq) * tm + lax.broadcasted_iota(jnp.int32, (tm, 1), 0)
    for gi, win in enumerate(POOL_WINDOWS):
        lo, hi = gi * V7X_LANES, (gi + 1) * V7X_LANES
        tot = ext_ref[POOL_HALO:, lo:hi]
        for d in range(1, win):
            tot = tot + ext_ref[POOL_HALO - d:POOL_HALO - d + tm, lo:hi]
        cnt = jnp.minimum(t_in_seq + 1, win).astype(F32)
        p = tot / cnt - c_ref[:, lo:hi]
        y = jnp.dot(p.astype(BF16), w_ref[gi], preferred_element_type=F32)
        o_ref[:, lo:hi] = (y * scale_ref[:, lo:hi]).astype(o_ref.dtype)


def pool_mixer(c_in, col, w, scale, S, tm=512):
    M = c_in.shape[0]
    return pl.pallas_call(
        functools.partial(_pool_kernel, tiles_per_seq=S // tm),
        grid=(M // tm,),
        in_specs=[pl.BlockSpec((tm, MIX), lambda i: (i, col)),
                  pl.BlockSpec((POOL_HALO, MIX), lambda i: (jnp.maximum(i * (tm // POOL_HALO) - 1, 0), col)),
                  pl.BlockSpec((len(POOL_WINDOWS), V7X_LANES, V7X_LANES), lambda i: (0, 0, 0)),
                  pl.BlockSpec((1, MIX), lambda i: (0, 0))],
        out_specs=pl.BlockSpec((tm, MIX), lambda i: (i, 0)),
        out_shape=jax.ShapeDtypeStruct((M, MIX), BF16),
        scratch_shapes=[pltpu.VMEM((tm + POOL_HALO, MIX), F32)],
        compiler_params=_params("parallel"),
        name="pool_mixer",
    )(c_in, c_in, w, scale.reshape(1, MIX))


def _sgu_kernel(d_ref, g_ref, b_ref, ws_ref, bs_ref, o_ref):
    causal = (lax.broadcasted_iota(jnp.int32, (SG_CHUNK, SG_CHUNK), 0)
              >= lax.broadcasted_iota(jnp.int32, (SG_CHUNK, SG_CHUNK), 1))
    hd = MIX // SG_H
    wm = [jnp.where(causal, ws_ref[g], 0.0).astype(BF16) for g in range(SG_H)]
    for c in range(d_ref.shape[0] // SG_CHUNK):
        rows = slice(c * SG_CHUNK, (c + 1) * SG_CHUNK)
        v = _ln(jax.nn.gelu(d_ref[rows, MIX:2 * MIX]), g_ref[...], b_ref[...]).astype(BF16)
        for g in range(SG_H):
            cols = slice(g * hd, (g + 1) * hd)
            vs = jnp.dot(wm[g], v[:, cols], preferred_element_type=F32) + bs_ref[:, g:g + 1]
            o_ref[rows, cols] = (jax.nn.gelu(d_ref[rows, cols]) * vs).astype(o_ref.dtype)


def sgu_mixer(d_in, col, ln_g, ln_b, ws, bs, chunks_per_step=4):
    M = d_in.shape[0]
    tm = chunks_per_step * SG_CHUNK
    return pl.pallas_call(
        _sgu_kernel,
        grid=(M // tm,),
        in_specs=[pl.BlockSpec((tm, 2 * MIX), lambda i: (i, col)),
                  pl.BlockSpec((1, MIX), lambda i: (0, 0)),
                  pl.BlockSpec((1, MIX), lambda i: (0, 0)),
                  pl.BlockSpec((SG_H, SG_CHUNK, SG_CHUNK), lambda i: (0, 0, 0)),
                  pl.BlockSpec((SG_CHUNK, SG_H), lambda i: (0, 0))],
        out_specs=pl.BlockSpec((tm, MIX), lambda i: (i, 0)),
        out_shape=jax.ShapeDtypeStruct((M, MIX), BF16),
        compiler_params=_params("parallel"),
        name="sgu_mixer",
    )(d_in, ln_g.reshape(1, MIX), ln_b.reshape(1, MIX), ws, bs.T)


def _xattn_kernel(q_ref, k_ref, v_ref, o_ref):
    for h in range(X_HEADS):
        sl = slice(h * X_DH, (h + 1) * X_DH)
        s = lax.dot_general(q_ref[:, sl], k_ref[:, sl], (((1,), (1,)), ((), ())),
                            preferred_element_type=F32) * (X_DH ** -0.5)
        p = jnp.exp(s - jnp.max(s, axis=-1, keepdims=True))
        p = p / jnp.sum(p, axis=-1, keepdims=True)
        o_ref[:, sl] = jnp.dot(p.astype(BF16), v_ref[:, sl], preferred_element_type=F32).astype(o_ref.dtype)


def cross_attention(q, k, v, S, mem_len, tm=512):
    M, W = q.shape
    return pl.pallas_call(
        _xattn_kernel,
        grid=(M // tm,),
        in_specs=[pl.BlockSpec((tm, W), lambda i: (i, 0)),
                  pl.BlockSpec((mem_len, W), lambda i: (i * tm // S, 0)),
                  pl.BlockSpec((mem_len, W), lambda i: (i * tm // S, 0))],
        out_specs=pl.BlockSpec((tm, W), lambda i: (i, 0)),
        out_shape=jax.ShapeDtypeStruct((M, W), BF16),
        compiler_params=_params("parallel"),
        name="cross_attention",
    )(q, k, v)


def _ffn_in_kernel(x_ref, halo_ref, wg32_ref, wu32_ref, cw_ref, cb_ref, o_ref, wg_ref, wu_ref, *, tiles_per_seq):
    @pl.when(pl.program_id(1) == 0)
    def _():
        wg_ref[...] = wg32_ref[...].astype(BF16)
        wu_ref[...] = wu32_ref[...].astype(BF16)

    first = pl.program_id(1) % tiles_per_seq == 0
    x = x_ref[...]
    g_main = jnp.dot(x, wg_ref[...], preferred_element_type=F32)
    g_halo = jnp.dot(halo_ref[...], wg_ref[...], preferred_element_type=F32)
    g_halo = jnp.where(first, 0.0, g_halo)
    ext = jnp.concatenate([g_halo, g_main], axis=0)
    n = ext.shape[0]
    conv = cb_ref[...] + cw_ref[CONV_W - 1:CONV_W] * g_main
    for d in range(1, CONV_W):
        conv = conv + cw_ref[CONV_W - 1 - d:CONV_W - d] * pltpu.roll(ext, d, axis=0)[CONV_HALO:n]
    up = jnp.dot(x, wu_ref[...], preferred_element_type=F32)
    o_ref[...] = (conv * jax.nn.sigmoid(conv) * up).astype(o_ref.dtype)


def ffn_in(xbf, w_in_all, conv_w_all, conv_b_all, layer, S, tm=1024, tn=512):
    M, D = xbf.shape
    F = w_in_all.shape[2] // 2
    nj = F // tn
    return pl.pallas_call(
        functools.partial(_ffn_in_kernel, tiles_per_seq=S // tm),
        grid=(nj, M // tm),
        in_specs=[pl.BlockSpec((tm, D), lambda j, i: (i, 0)),
                  pl.BlockSpec((CONV_HALO, D), lambda j, i: (jnp.maximum(i * (tm // CONV_HALO) - 1, 0), 0)),
                  pl.BlockSpec((None, D, tn), lambda j, i: (layer, 0, j)),
                  pl.BlockSpec((None, D, tn), lambda j, i: (layer, 0, j + nj)),
                  pl.BlockSpec((None, CONV_W, tn), lambda j, i: (layer, 0, j)),
                  pl.BlockSpec((None, 1, tn), lambda j, i: (layer, 0, j))],
        out_specs=pl.BlockSpec((tm, tn), lambda j, i: (i, j)),
        out_shape=jax.ShapeDtypeStruct((M, F), BF16),
        scratch_shapes=[pltpu.VMEM((D, tn), BF16), pltpu.VMEM((D, tn), BF16)],
        compiler_params=_params("parallel", "arbitrary"),
        name="ffn_in",
    )(xbf, xbf, w_in_all, w_in_all, conv_w_all, conv_b_all.reshape(conv_b_all.shape[0], 1, F))


def _nsa_weights(w_q, w_kv, w_gate):
    D = w_q.shape[0]
    gap = jnp.zeros((D, V7X_LANES - NSA_DH), F32)
    per_branch = 2 * NSA_G * NSA_DH
    k_cols, v_cols = [], []
    for br in (1, 2):
        base = br * per_branch
        for g in range(NSA_G):
            k_cols += [w_kv[:, base + g * NSA_DH:base + (g + 1) * NSA_DH], gap]
        v_cols.append(w_kv[:, base + NSA_G * NSA_DH:base + per_branch])
    gate = jnp.pad(w_gate, ((0, 0), (0, V7X_LANES - w_gate.shape[1])))
    return jnp.concatenate([w_q, gate, w_kv[:, :per_branch]] + k_cols + v_cols, axis=1).astype(BF16)


def _compress_weights(cmp_pos, cmp_w):
    half = NSA_CMP_LEN // 2
    eye_g = jnp.eye(NSA_G, dtype=F32)
    ws, ps = [], []
    for part in range(2):
        blocks = []
        for kv in range(2):
            w = cmp_w[kv, part * half:(part + 1) * half]
            full = jnp.einsum('lde,gh->lgdhe', w, eye_g)
            z = jnp.zeros_like(full)
            pair = [full, z] if kv == 0 else [z, full]
            blocks.append(jnp.stack(pair, axis=3))
        w_all = jnp.stack(blocks, axis=1)
        w_all = w_all.reshape(half * 2 * NSA_G * NSA_DH, 2 * NSA_G * NSA_DH)
        gap = jnp.zeros((w_all.shape[0], V7X_LANES - NSA_DH), F32)
        w_all = jnp.concatenate([w_all[:, :NSA_DH], gap, w_all[:, NSA_DH:2 * NSA_DH], gap,
                                 w_all[:, 2 * NSA_DH:]], axis=1)
        ws.append(w_all.astype(BF16))
        p = cmp_pos[:, part * half:(part + 1) * half]
        p = jnp.broadcast_to(p.transpose(1, 0, 2)[:, :, None, :], (half, 2, NSA_G, NSA_DH))
        ps.append(p.reshape(-1))
    return jnp.stack(ps, axis=0), ws[0], ws[1]


def kernel(x, mem, w_in, nsa_cmp_pos, nsa_cmp_w, hgrn_lb_logits, hgrn_norm_g, pool_w, pool_scale, sg_ln_g, sg_ln_b, sg_w, sg_b, w_branch, w_mix_out, ln_mix_g, ln_mix_b, mem_ln_g, mem_ln_b, xattn_q, xattn_k, xattn_v, xattn_o, ln_x_g, ln_x_b, ffn_in_w, ffn_conv_w, ffn_conv_b, ffn_out, ln_ffn_g, ln_ffn_b):
    B, S, D = x.shape
    M = B * S
    mem_len = mem.shape[1]
    assert D == D_MODEL and S % 512 == 0 and S // NSA_SEL_BLOCK <= V7X_LANES

    sm = jax.nn.softmax(hgrn_lb_logits.astype(F32), axis=0)
    lower_bounds = jnp.cumsum(sm, axis=0) - sm[0]
    sel_map_t = _sel_map_t(S)
    mem_n = layer_norm_rows(mem.reshape(B * mem_len, D), mem_ln_g, mem_ln_b, BF16)

    xf = x.reshape(M, D)
    xbf = xf.astype(BF16)
    o0, o1, o2, o3, o4, o5 = np.cumsum([MIX, 768, 24, 4 * MIX, MIX, 2 * MIX])
    for l in range(DEPTH):
        w = w_in[l]
        w_nsa = _nsa_weights(w[:, :o0], w[:, o0:o1], w[:, o1:o2])
        w_bdc = jnp.concatenate([w[:, o2:o3], w[:, o4:o5], w[:, o3:o4]], axis=1).astype(BF16)
        w_merge = w[:, o5:].astype(BF16)

        a_q, a_g, kv_cmp, k_sw, vt_sw = nsa_project(xbf, w_nsa, S)
        bdc = matmul_resident(xbf, w_bdc, F32)

        pos2, wlo, whi = _compress_weights(nsa_cmp_pos[l], nsa_cmp_w[l])
        kc, vct = nsa_compress(kv_cmp, pos2, wlo, whi, B, S)
        o_a = nsa_attention(a_q, a_g, k_sw, vt_sw, kc, vct, sel_map_t, B, S)
        o_b = hgrn2(bdc, lower_bounds[l], hgrn_norm_g[l], B, S)
        o_c = pool_mixer(bdc, 6, pool_w[l].astype(BF16), pool_scale[l], S)
        o_d = sgu_mixer(bdc, 2, sg_ln_g[l], sg_ln_b[l], sg_w[l], sg_b[l])

        gated = merge_branches(xbf, (o_a, o_b, o_c, o_d), w_merge, w_branch[l].astype(BF16))
        xf, xbf = matmul_residual_ln(gated, w_mix_out[l].astype(BF16), xf, ln_mix_g[l], ln_mix_b[l])

        xq = matmul_resident(xbf, xattn_q[l].astype(BF16), BF16, tm=512)
        xk = matmul(mem_n, xattn_k[l].astype(BF16), BF16)
        xv = matmul(mem_n, xattn_v[l].astype(BF16), BF16)
        xo = cross_attention(xq, xk, xv, S, mem_len)
        xf, xbf = matmul_residual_ln(xo, xattn_o[l].astype(BF16), xf, ln_x_g[l], ln_x_b[l])

        act = ffn_in(xbf, ffn_in_w, ffn_conv_w, ffn_conv_b, l, S)
        xf, xbf = matmul_residual_ln(act, ffn_out[l].astype(BF16), xf, ln_ffn_g[l], ln_ffn_b[l])
    return xf.reshape(B, S, D)
```

```python
import functools

import numpy as np
import jax
import jax.numpy as jnp
from jax import lax
from jax.experimental import pallas as pl
from jax.experimental.pallas import tpu as pltpu

F32 = jnp.float32
BF16 = jnp.bfloat16

V7X_LANES = 128
V7X_VMEM_BYTES = 64 * 1024 * 1024
VMEM_LIMIT = 48 * 1024 * 1024

D_MODEL = 2048
DEPTH = 2
MIX = D_MODEL // 4
N_BRANCH = 4

NSA_DH = 64
NSA_G = 2
NSA_J = 4
NSA_HEADS = NSA_G * NSA_J
NSA_CMP_LEN = 32
NSA_CMP_STRIDE = 16
NSA_SEL_BLOCK = 64
NSA_N_SEL = 16
NSA_WINDOW = 512
NSA_T = 128
NSA_TK = 512
NSA_LROWS = 16
NSA_FORCE = 1e6

HG_H = 4
HG_DK = 128
HG_C = 64
HG_SUB = 16
HG_FMIN = 1e-6

POOL_WINDOWS = (2, 4, 8, 16)
POOL_HALO = 16

SG_CHUNK = 128
SG_H = 4

X_HEADS = 4
X_DH = 128

D_FF = 5632
CONV_W = 3
CONV_HALO = 16

LN_EPS = 1e-5
NEG = -1e30
ALPHA = (2 * DEPTH) ** 0.25

ALIBI = [float(2.0 ** (-8.0 * (h + 1) / NSA_HEADS)) for h in range(NSA_HEADS)]


def _params(*sem):
    return pltpu.CompilerParams(dimension_semantics=sem, vmem_limit_bytes=VMEM_LIMIT)


def _ln(y, g, b):
    mu = jnp.mean(y, axis=-1, keepdims=True)
    d = y - mu
    var = jnp.mean(d * d, axis=-1, keepdims=True)
    return d * lax.rsqrt(var + LN_EPS) * g + b


def _mm_kernel(a_ref, w_ref, o_ref):
    o_ref[...] = jnp.dot(a_ref[...], w_ref[...], preferred_element_type=F32).astype(o_ref.dtype)


def matmul(a, w, out_dtype, tm=512):
    M, K = a.shape
    N = w.shape[1]
    tm = min(tm, M)
    tn = min(N, 512)
    return pl.pallas_call(
        _mm_kernel,
        grid=(N // tn, M // tm),
        in_specs=[pl.BlockSpec((tm, K), lambda j, i: (i, 0)),
                  pl.BlockSpec((K, tn), lambda j, i: (0, j))],
        out_specs=pl.BlockSpec((tm, tn), lambda j, i: (i, j)),
        out_shape=jax.ShapeDtypeStruct((M, N), out_dtype),
        compiler_params=_params("parallel", "parallel"),
        name="matmul",
    )(a, w)


def _mm_resident_kernel(a_ref, w_ref, o_ref, *, tn):
    a = a_ref[...]
    for n0 in range(0, w_ref.shape[1], tn):
        o_ref[:, n0:n0 + tn] = jnp.dot(a, w_ref[:, n0:n0 + tn], preferred_element_type=F32).astype(o_ref.dtype)


def matmul_resident(a, w, out_dtype, tm=256, tn=512):
    M, K = a.shape
    N = w.shape[1]
    return pl.pallas_call(
        functools.partial(_mm_resident_kernel, tn=tn),
        grid=(M // tm,),
        in_specs=[pl.BlockSpec((tm, K), lambda i: (i, 0)),
                  pl.BlockSpec((K, N), lambda i: (0, 0))],
        out_specs=pl.BlockSpec((tm, N), lambda i: (i, 0)),
        out_shape=jax.ShapeDtypeStruct((M, N), out_dtype),
        compiler_params=_params("parallel"),
        name="matmul_resident",
    )(a, w)


def _mm_ln_kernel(a_ref, w_ref, x_ref, g_ref, b_ref, o_ref, obf_ref, *scratch, n_k):
    part = jnp.dot(a_ref[...], w_ref[...], preferred_element_type=F32)

    def finish(total):
        y = _ln(ALPHA * x_ref[...] + total, g_ref[...], b_ref[...])
        o_ref[...] = y
        obf_ref[...] = y.astype(BF16)

    if n_k == 1:
        finish(part)
        return
    acc_ref, = scratch
    k = pl.program_id(1)

    @pl.when(k == 0)
    def _():
        acc_ref[...] = part

    @pl.when((k > 0) & (k < n_k - 1))
    def _():
        acc_ref[...] += part

    @pl.when(k == n_k - 1)
    def _():
        finish(acc_ref[...] + part)


MM_LN_WEIGHT_TILE_BYTES = 16 * 1024 * 1024


def matmul_residual_ln(a, w, x, g, b, tm=512):
    M, K = a.shape
    D = w.shape[1]
    n_k = next(n for n in range(1, K) if K % n == 0 and (K // n) % V7X_LANES == 0
               and (K // n) * D * 2 * 2 <= MM_LN_WEIGHT_TILE_BYTES)
    tk = K // n_k
    return pl.pallas_call(
        functools.partial(_mm_ln_kernel, n_k=n_k),
        grid=(M // tm, n_k),
        in_specs=[pl.BlockSpec((tm, tk), lambda i, k: (i, k)),
                  pl.BlockSpec((tk, D), lambda i, k: (k, 0)),
                  pl.BlockSpec((tm, D), lambda i, k: (i, 0)),
                  pl.BlockSpec((1, D), lambda i, k: (0, 0)),
                  pl.BlockSpec((1, D), lambda i, k: (0, 0))],
        out_specs=[pl.BlockSpec((tm, D), lambda i, k: (i, 0)),
                   pl.BlockSpec((tm, D), lambda i, k: (i, 0))],
        out_shape=[jax.ShapeDtypeStruct((M, D), F32), jax.ShapeDtypeStruct((M, D), BF16)],
        scratch_shapes=[pltpu.VMEM((tm, D), F32)] if n_k > 1 else [],
        compiler_params=_params("parallel", "arbitrary"),
        name="matmul_residual_ln",
    )(a, w, x, g.reshape(1, D), b.reshape(1, D))


def _ln_kernel(x_ref, g_ref, b_ref, o_ref):
    o_ref[...] = _ln(x_ref[...], g_ref[...], b_ref[...]).astype(o_ref.dtype)


def layer_norm_rows(x, g, b, out_dtype, tm=256):
    M, D = x.shape
    return pl.pallas_call(
        _ln_kernel,
        grid=(M // tm,),
        in_specs=[pl.BlockSpec((tm, D), lambda i: (i, 0)),
                  pl.BlockSpec((1, D), lambda i: (0, 0)),
                  pl.BlockSpec((1, D), lambda i: (0, 0))],
        out_specs=pl.BlockSpec((tm, D), lambda i: (i, 0)),
        out_shape=jax.ShapeDtypeStruct((M, D), out_dtype),
        compiler_params=_params("parallel"),
        name="layer_norm_rows",
    )(x, g.reshape(1, D), b.reshape(1, D))


def _merge_kernel(x_ref, *refs):
    o_refs, wm_refs, (wb_ref, o_ref) = refs[:N_BRANCH], refs[N_BRANCH:2 * N_BRANCH], refs[2 * N_BRANCH:]
    x = x_ref[...]
    acc = None
    for n in range(N_BRANCH):
        gate = jax.nn.sigmoid(jnp.dot(x, wm_refs[n][...], preferred_element_type=F32))
        y = jnp.dot(o_refs[n][...], wb_ref[n], preferred_element_type=F32)
        acc = gate * y if acc is None else acc + gate * y
    o_ref[...] = acc.astype(o_ref.dtype)


def merge_branches(xbf, branches, w_merge, w_branch, tm=512, tn=512):
    M, D = xbf.shape
    W = branches[0].shape[1]
    return pl.pallas_call(
        _merge_kernel,
        grid=(D // tn, M // tm),
        in_specs=[pl.BlockSpec((tm, D), lambda j, i: (i, 0))]
        + [pl.BlockSpec((tm, W), lambda j, i: (i, 0))] * N_BRANCH
        + [pl.BlockSpec((D, tn), lambda j, i, n=n: (0, n * (D // tn) + j)) for n in range(N_BRANCH)]
        + [pl.BlockSpec((N_BRANCH, W, tn), lambda j, i: (0, 0, j))],
        out_specs=pl.BlockSpec((tm, tn), lambda j, i: (i, j)),
        out_shape=jax.ShapeDtypeStruct((M, D), BF16),
        compiler_params=_params("parallel", "parallel"),
        name="merge_branches",
    )(xbf, *branches, *([w_merge] * N_BRANCH), w_branch)


def _aux_lanes(k, lane, vals):
    for i, v in enumerate(vals):
        k = jnp.where(lane == NSA_DH + i, v, k)
    return k


def _compress_kernel(rows_ref, pos_ref, wlo_ref, whi_ref, kc_ref, vct_ref):
    rows = rows_ref[0]
    lo = jnp.dot((rows + pos_ref[0:1]).astype(BF16), wlo_ref[...], preferred_element_type=F32)
    hi = jnp.dot((rows + pos_ref[1:2]).astype(BF16), whi_ref[...], preferred_element_type=F32)
    n = rows.shape[0]
    res = lo + pltpu.roll(hi, n - 1, axis=0)
    kw = NSA_G * V7X_LANES
    c = lax.broadcasted_iota(jnp.int32, (n, kw), 0)
    lane = lax.broadcasted_iota(jnp.int32, (n, kw), 1) % V7X_LANES
    kc = _aux_lanes(res[:, :kw], lane, [(c // 8).astype(F32), (c % 8).astype(F32), 1.0, 1.0])
    kc_ref[0] = kc.astype(kc_ref.dtype)
    vct_ref[0] = res[:, kw:].T.astype(vct_ref.dtype)


def nsa_compress(kv_cmp, pos2, wlo, whi, B, S):
    R = S // NSA_CMP_STRIDE
    C = kv_cmp.shape[1]
    CO = wlo.shape[1]
    rows = kv_cmp.reshape(B, R, NSA_CMP_STRIDE * C)
    return pl.pallas_call(
        _compress_kernel,
        grid=(B,),
        in_specs=[pl.BlockSpec((1, R, NSA_CMP_STRIDE * C), lambda b: (b, 0, 0)),
                  pl.BlockSpec((2, NSA_CMP_STRIDE * C), lambda b: (0, 0)),
                  pl.BlockSpec((NSA_CMP_STRIDE * C, CO), lambda b: (0, 0)),
                  pl.BlockSpec((NSA_CMP_STRIDE * C, CO), lambda b: (0, 0))],
        out_specs=[pl.BlockSpec((1, R, NSA_G * V7X_LANES), lambda b: (b, 0, 0)),
                   pl.BlockSpec((1, V7X_LANES, R), lambda b: (b, 0, 0))],
        out_shape=[jax.ShapeDtypeStruct((B, R, NSA_G * V7X_LANES), BF16),
                   jax.ShapeDtypeStruct((B, V7X_LANES, R), BF16)],
        compiler_params=_params("parallel"),
        name="nsa_compress",
    )(rows, pos2, wlo, whi)


def _nsa_proj_kernel(x_ref, w_ref, q_ref, gl_ref, kvc_ref, k_ref, vt_ref, *, tiles_per_seq):
    tm = x_ref.shape[0]
    h = jnp.dot(x_ref[...], w_ref[...], preferred_element_type=F32)
    c0 = MIX
    c1 = c0 + V7X_LANES
    c2 = c1 + 2 * NSA_G * NSA_DH
    c3 = c2 + 2 * NSA_G * V7X_LANES
    q_ref[...] = h[:, :c0].astype(q_ref.dtype)
    gl_ref[...] = h[:, c0:c1]
    kvc_ref[...] = h[:, c1:c2]
    pos = (pl.program_id(0) % tiles_per_seq) * tm + lax.broadcasted_iota(jnp.int32, (tm, c3 - c2), 0)
    lane = lax.broadcasted_iota(jnp.int32, (tm, c3 - c2), 1) % V7X_LANES
    k = _aux_lanes(h[:, c2:c3], lane, [(pos // NSA_T).astype(F32), (pos % NSA_T).astype(F32), 1.0])
    k_ref[...] = k.astype(k_ref.dtype)
    vt_ref[...] = h[:, c3:].T.astype(vt_ref.dtype)


def nsa_project(xbf, w_all, S, tm=512):
    M, D = xbf.shape
    N = w_all.shape[1]
    kw = 2 * NSA_G * V7X_LANES
    return pl.pallas_call(
        functools.partial(_nsa_proj_kernel, tiles_per_seq=S // tm),
        grid=(M // tm,),
        in_specs=[pl.BlockSpec((tm, D), lambda i: (i, 0)),
                  pl.BlockSpec((D, N), lambda i: (0, 0))],
        out_specs=[pl.BlockSpec((tm, MIX), lambda i: (i, 0)),
                   pl.BlockSpec((tm, V7X_LANES), lambda i: (i, 0)),
                   pl.BlockSpec((tm, 2 * NSA_G * NSA_DH), lambda i: (i, 0)),
                   pl.BlockSpec((tm, kw), lambda i: (i, 0)),
                   pl.BlockSpec((2 * V7X_LANES, tm), lambda i: (0, i))],
        out_shape=[jax.ShapeDtypeStruct((M, MIX), BF16),
                   jax.ShapeDtypeStruct((M, V7X_LANES), F32),
                   jax.ShapeDtypeStruct((M, 2 * NSA_G * NSA_DH), F32),
                   jax.ShapeDtypeStruct((M, kw), BF16),
                   jax.ShapeDtypeStruct((2 * V7X_LANES, M), BF16)],
        compiler_params=_params("parallel"),
        name="nsa_project",
    )(xbf, w_all)


def _nsa_kernel(q_ref, gl_ref, ks_ref, vst_ref, kw_ref, vwt_ref, kc_ref, vct_ref, selmap_ref, o_ref,
                qt_ref, qtc_ref, selneg_ref, p_ref, acc_ref, res_ref, tiles_ref, *, n_blk):
    T, TK, DH = NSA_T, NSA_TK, NSA_DH
    qb = pl.program_id(1)
    q0 = qb * T
    t_row = q0 + lax.broadcasted_iota(jnp.int32, (1, T), 1)
    aux_row = lax.broadcasted_iota(jnp.int32, (DH, T), 0)

    qf = q_ref[...].astype(F32) * (DH ** -0.5)
    q_t = [qf[:, c * T:(c + 1) * T].T for c in range(NSA_HEADS // 2)]
    for g in range(NSA_G):
        main, cmp = [], []
        for j in range(NSA_J):
            h = g * NSA_J + j
            sl = ALIBI[h]
            off = (-sl * T) * qb.astype(F32)
            aux = jnp.where(aux_row == 0, sl * T, jnp.where(aux_row == 2, off, 0.0))
            aux_main = jnp.where(aux_row == 1, sl, aux)
            aux_cmp = jnp.where(aux_row == 1, sl * NSA_CMP_STRIDE,
                                jnp.where(aux_row == 3, sl * (NSA_CMP_LEN - 1), aux))
            qh = q_t[h // 2][(h % 2) * DH:(h % 2 + 1) * DH]
            main.append(jnp.concatenate([qh, aux_main], axis=0))
            cmp.append(jnp.concatenate([qh, aux_cmp], axis=0))
        qt_ref[g] = jnp.concatenate(main, axis=1).astype(BF16)
        qtc_ref[g] = jnp.concatenate(cmp, axis=1).astype(BF16)

    def per_head(x):
        return jnp.concatenate([x] * NSA_J, axis=1)

    n_c = kc_ref.shape[1]
    c_col = lax.broadcasted_iota(jnp.int32, (n_c, T), 0)
    negb_c = per_head(jnp.where(c_col * NSA_CMP_STRIDE + (NSA_CMP_LEN - 1) <= t_row, 0.0, NEG))
    has_key = per_head(t_row >= NSA_CMP_LEN - 1)
    blk = lax.broadcasted_iota(jnp.int32, (V7X_LANES, T), 0)
    blk_f = blk.astype(F32)
    cur = t_row // NSA_SEL_BLOCK
    forced = (blk == 0) | (blk == cur) | (blk == cur - 1)
    future = blk * NSA_SEL_BLOCK > t_row
    for g in range(NSA_G):
        kc = kc_ref[0, :, g * V7X_LANES:(g + 1) * V7X_LANES]
        vct = vct_ref[0, g * DH:(g + 1) * DH, :]
        s = jnp.dot(kc, qtc_ref[g], preferred_element_type=F32) + negb_c
        p = jnp.exp(s - jnp.max(s, axis=0, keepdims=True))
        l = jnp.sum(p, axis=0, keepdims=True)
        p = p * jnp.where(has_key, 1.0 / l, 0.0)
        res_ref[0, g] = jnp.dot(vct, p.astype(BF16), preferred_element_type=F32)
        psum = sum(p[:, j * T:(j + 1) * T] for j in range(NSA_J))
        imp = jnp.dot(selmap_ref[...], psum, preferred_element_type=F32, precision=lax.Precision.HIGHEST)
        work = jnp.where(forced, NSA_FORCE, jnp.where(future, NEG, imp))
        if n_blk < V7X_LANES:
            work = jnp.where(blk >= n_blk, -jnp.inf, work)
        selneg = jnp.full((V7X_LANES, T), NEG, F32)
        for _ in range(min(NSA_N_SEL, n_blk)):
            mx = jnp.max(work, axis=0, keepdims=True)
            first = jnp.min(jnp.where(work == mx, blk_f, float(V7X_LANES)), axis=0, keepdims=True)
            hit = blk_f == first
            selneg = jnp.where(hit, 0.0, selneg)
            work = jnp.where(hit, -jnp.inf, work)
        selneg_ref[g] = selneg

    key_in_tile = lax.broadcasted_iota(jnp.int32, (TK, T), 0)
    ones_rows = jnp.ones((NSA_LROWS, TK), BF16)
    last = qb // (TK // T)

    per_tile = TK // NSA_SEL_BLOCK
    tile_bits = jnp.zeros((1, 1), F32)
    for kt in range(n_blk // per_tile):
        rows = slice(kt * per_tile, (kt + 1) * per_tile)
        picked = jnp.where((selneg_ref[0, rows, :] == 0.0) | (selneg_ref[1, rows, :] == 0.0), 1.0, 0.0)
        tile_bits = tile_bits + jnp.max(jnp.max(picked, axis=1, keepdims=True), axis=0, keepdims=True) * float(2 ** kt)
    tile_mask = tile_bits[0, 0].astype(jnp.int32)
    n_sel_tiles = jnp.int32(0)
    for kt in range(n_blk // per_tile):
        needed = ((((tile_mask >> kt) & 1) == 1) & (kt < last)) | (kt == last)
        tiles_ref[n_sel_tiles] = jnp.int32(kt)
        n_sel_tiles = n_sel_tiles + needed.astype(jnp.int32)

    def flash(k_ref, vt_ref, n, tile_at, negb_loop, negb_last, branch):
        def scores(kt, negb_fn):
            k0 = pl.multiple_of(kt * TK, TK)
            return [jnp.dot(k_ref[pl.ds(k0, TK), g * V7X_LANES:(g + 1) * V7X_LANES], qt_ref[g],
                            preferred_element_type=F32) + per_head(negb_fn(g, kt, k0)) for g in range(NSA_G)]

        def fold_values(kt, a):
            k0 = pl.multiple_of(kt * TK, TK)
            for g in range(NSA_G):
                vt = jnp.concatenate([vt_ref[g * DH:(g + 1) * DH, pl.ds(k0, TK)], ones_rows], axis=0)
                acc_ref[g] = a[g] * acc_ref[g] + jnp.dot(vt, p_ref[g], preferred_element_type=F32)

        def softmax(s, m_old):
            m_new, a = [], []
            for g in range(NSA_G):
                m_new.append(jnp.maximum(m_old[g], jnp.max(s[g], axis=0, keepdims=True)))
                p_ref[g] = jnp.exp((s[g] - m_new[g]).astype(BF16))
                a.append(jnp.exp(m_old[g] - m_new[g]))
            return tuple(m_new), tuple(a)

        def step(i, carry, negb_fn):
            m, a = carry
            s = scores(tile_at(i), negb_fn)
            fold_values(tile_at(i - 1), a)
            return softmax(s, m)

        acc_ref[...] = jnp.zeros(acc_ref.shape, F32)
        m_init = tuple(jnp.full((1, NSA_J * T), NEG, F32) for _ in range(NSA_G))
        carry = softmax(scores(tile_at(0), negb_last), m_init)
        carry = lax.fori_loop(1, n - 1, lambda i, c: step(i, c, negb_loop), carry)
        carry = lax.cond(n > 1, lambda c: step(n - 1, c, negb_last), lambda c: c, carry)
        fold_values(last, carry[1])
        for g in range(NSA_G):
            res_ref[branch, g] = acc_ref[g, :DH] / acc_ref[g, DH:DH + 1]

    def sel_negb(causal):
        def fn(g, kt, k0):
            nb = jnp.concatenate(
                [jnp.broadcast_to(selneg_ref[g, pl.ds(kt * per_tile + i, 1), :], (NSA_SEL_BLOCK, T))
                 for i in range(per_tile)], axis=0)
            if causal:
                nb = jnp.where(k0 + key_in_tile <= t_row, nb, NEG)
            return nb
        return fn

    def win_negb(g, kt, k0):
        dist = t_row - (k0 + key_in_tile)
        return jnp.where((dist >= 0) & (dist < NSA_WINDOW), 0.0, NEG)

    flash(ks_ref, vst_ref, n_sel_tiles, lambda i: tiles_ref[i], sel_negb(False), sel_negb(True), 1)
    win_lo = jnp.maximum(q0 - (NSA_WINDOW - 1), 0) // TK
    flash(kw_ref, vwt_ref, last - win_lo + 1, lambda i: win_lo + i, win_negb, win_negb, 2)

    g_t = jax.nn.sigmoid(gl_ref[...]).T
    for c in range(NSA_HEADS // 2):
        halves = []
        for e in range(2):
            h = 2 * c + e
            g, j = divmod(h, NSA_J)
            halves.append(sum(g_t[3 * h + br:3 * h + br + 1] * res_ref[br, g, :, j * T:(j + 1) * T]
                              for br in range(3)))
        o_ref[:, c * T:(c + 1) * T] = jnp.concatenate(halves, axis=0).T.astype(o_ref.dtype)


def nsa_attention(q, gate_logits, k_sw, vt_sw, kc, vct, sel_map_t, B, S):
    nq = S // NSA_T
    R = kc.shape[1]
    kw = NSA_G * V7X_LANES
    return pl.pallas_call(
        functools.partial(_nsa_kernel, n_blk=S // NSA_SEL_BLOCK),
        grid=(B, nq),
        in_specs=[pl.BlockSpec((NSA_T, MIX), lambda b, i: (b * nq + i, 0)),
                  pl.BlockSpec((NSA_T, V7X_LANES), lambda b, i: (b * nq + i, 0)),
                  pl.BlockSpec((S, kw), lambda b, i: (b, 0)),
                  pl.BlockSpec((V7X_LANES, S), lambda b, i: (0, b)),
                  pl.BlockSpec((S, kw), lambda b, i: (b, 1)),
                  pl.BlockSpec((V7X_LANES, S), lambda b, i: (1, b)),
                  pl.BlockSpec((1, R, kw), lambda b, i: (b, 0, 0)),
                  pl.BlockSpec((1, V7X_LANES, R), lambda b, i: (b, 0, 0)),
                  pl.BlockSpec((V7X_LANES, R), lambda b, i: (0, 0))],
        out_specs=pl.BlockSpec((NSA_T, MIX), lambda b, i: (b * nq + i, 0)),
        out_shape=jax.ShapeDtypeStruct((B * S, MIX), BF16),
        scratch_shapes=[pltpu.VMEM((NSA_G, V7X_LANES, NSA_J * NSA_T), BF16),
                        pltpu.VMEM((NSA_G, V7X_LANES, NSA_J * NSA_T), BF16),
                        pltpu.VMEM((NSA_G, V7X_LANES, NSA_T), F32),
                        pltpu.VMEM((NSA_G, NSA_TK, NSA_J * NSA_T), BF16),
                        pltpu.VMEM((NSA_G, NSA_DH + NSA_LROWS, NSA_J * NSA_T), F32),
                        pltpu.VMEM((3, NSA_G, NSA_DH, NSA_J * NSA_T), F32),
                        pltpu.SMEM((S // NSA_TK + 1,), jnp.int32)],
        compiler_params=_params("parallel", "parallel"),
        name="nsa_attention",
    )(q, gate_logits, k_sw, vt_sw, k_sw, vt_sw, kc, vct, sel_map_t)


def _sel_map_t(S):
    n_c = S // NSA_CMP_STRIDE
    c0 = np.arange(n_c)[None, :] * NSA_CMP_STRIDE
    s0 = np.arange(V7X_LANES)[:, None] * NSA_SEL_BLOCK
    ov = np.clip(np.minimum(c0 + NSA_CMP_LEN, s0 + NSA_SEL_BLOCK) - np.maximum(c0, s0), 0, None)
    return jnp.asarray(ov / NSA_CMP_LEN, dtype=F32)


def _hgrn_kernel(x_ref, lb_ref, ng_ref, o_ref, state_ref):
    C, dk, U = HG_C, HG_DK, HG_SUB

    @pl.when(pl.program_id(0) == 0)
    def _():
        state_ref[...] = jnp.zeros_like(state_ref)

    tri = (lax.broadcasted_iota(jnp.int32, (C, C), 0) >= lax.broadcasted_iota(jnp.int32, (C, C), 1)).astype(F32)
    rows = lax.broadcasted_iota(jnp.int32, (U, 1), 0)
    for bh in range(x_ref.shape[0] * HG_H):
        bi, h = divmod(bh, HG_H)

        def col(part):
            return x_ref[bi, :, part * MIX + h * dk: part * MIX + (h + 1) * dk]
        q, z, v, gate = col(0), col(1), col(2), col(3)
        lb = lb_ref[:, h * dk:(h + 1) * dk]
        f = jnp.maximum(lb + (1.0 - lb) * jax.nn.sigmoid(z), HG_FMIN)
        kk = 1.0 - f
        b = jnp.dot(tri, jnp.log(f), preferred_element_type=F32, precision=lax.Precision.HIGHEST)
        st = state_ref[bh]
        o = lax.dot_general((q * jnp.exp(b)).astype(BF16), st.astype(BF16), (((1,), (1,)), ((), ())),
                            preferred_element_type=F32)
        v_bf = v.astype(BF16)
        parts = []
        for i in range(C // U):
            r0 = i * U
            b_r, q_r, o_r = b[r0:r0 + U], q[r0:r0 + U], o[r0:r0 + U]
            if i > 0:
                b_e = b[r0 - 1:r0]
                qe = (q_r * jnp.exp(b_r - b_e)).astype(BF16)
                ke = (kk[0:r0] * jnp.exp(b_e - b[0:r0])).astype(BF16)
                a = lax.dot_general(qe, ke, (((1,), (1,)), ((), ())), preferred_element_type=F32)
                o_r = o_r + jnp.dot(a.astype(BF16), v_bf[0:r0], preferred_element_type=F32)
            for s in range(U):
                d = jnp.where(rows >= s, b_r - b[r0 + s:r0 + s + 1], NEG)
                a = jnp.sum(q_r * jnp.exp(d) * kk[r0 + s:r0 + s + 1], axis=-1, keepdims=True)
                o_r = o_r + a * v[r0 + s:r0 + s + 1]
            parts.append(o_r)
        o = jnp.concatenate(parts, axis=0)
        b_last = b[C - 1:C]
        kd = (kk * jnp.exp(b_last - b)).astype(BF16)
        state_ref[bh] = st * jnp.exp(b_last) + jnp.dot(v.T.astype(BF16), kd, preferred_element_type=F32)
        y = o * lax.rsqrt(jnp.mean(o * o, axis=-1, keepdims=True) + LN_EPS) * ng_ref[:, h * dk:(h + 1) * dk]
        o_ref[bi, :, h * dk:(h + 1) * dk] = (y * (gate * jax.nn.sigmoid(gate))).astype(o_ref.dtype)


def hgrn2(b_in, lb, norm_g, B, S):
    n = S // HG_C
    width = b_in.shape[1]
    out = pl.pallas_call(
        _hgrn_kernel,
        grid=(n,),
        in_specs=[pl.BlockSpec((B, HG_C, 4 * MIX), lambda c: (0, c, 0)),
                  pl.BlockSpec((1, MIX), lambda c: (0, 0)),
                  pl.BlockSpec((1, MIX), lambda c: (0, 0))],
        out_specs=pl.BlockSpec((B, HG_C, MIX), lambda c: (0, c, 0)),
        out_shape=jax.ShapeDtypeStruct((B, S, MIX), BF16),
        scratch_shapes=[pltpu.VMEM((B * HG_H, HG_DK, HG_DK), F32)],
        compiler_params=_params("arbitrary"),
        name="hgrn2",
    )(b_in.reshape(B, S, width), lb.reshape(1, MIX), norm_g.reshape(1, MIX))
    return out.reshape(B * S, MIX)


def _pool_kernel(c_ref, halo_ref, w_ref, scale_ref, o_ref, ext_ref, *, tiles_per_seq):
    tm = c_ref.shape[0]
    first = pl.program_id(0) % tiles_per_seq == 0
    ext_ref[0:POOL_HALO, :] = jnp.where(first, 0.0, halo_ref[...])
    ext_ref[POOL_HALO:, :] = c_ref[...]
    t_in_seq = (pl.program_id(0) % tiles_per_seq) * tm + lax.broadcasted_iota(jnp.int32, (tm, 1), 0)
    for gi, win in enumerate(POOL_WINDOWS):
        lo, hi = gi * V7X_LANES, (gi + 1) * V7X_LANES
        tot = ext_ref[POOL_HALO:, lo:hi]
        for d in range(1, win):
            tot = tot + ext_ref[POOL_HALO - d:POOL_HALO - d + tm, lo:hi]
        cnt = jnp.minimum(t_in_seq + 1, win).astype(F32)
        p = tot / cnt - c_ref[:, lo:hi]
        y = jnp.dot(p.astype(BF16), w_ref[gi], preferred_element_type=F32)
        o_ref[:, lo:hi] = (y * scale_ref[:, lo:hi]).astype(o_ref.dtype)


def pool_mixer(c_in, col, w, scale, S, tm=512):
    M = c_in.shape[0]
    return pl.pallas_call(
        functools.partial(_pool_kernel, tiles_per_seq=S // tm),
        grid=(M // tm,),
        in_specs=[pl.BlockSpec((tm, MIX), lambda i: (i, col)),
                  pl.BlockSpec((POOL_HALO, MIX), lambda i: (jnp.maximum(i * (tm // POOL_HALO) - 1, 0), col)),
                  pl.BlockSpec((len(POOL_WINDOWS), V7X_LANES, V7X_LANES), lambda i: (0, 0, 0)),
                  pl.BlockSpec((1, MIX), lambda i: (0, 0))],
        out_specs=pl.BlockSpec((tm, MIX), lambda i: (i, 0)),
        out_shape=jax.ShapeDtypeStruct((M, MIX), BF16),
        scratch_shapes=[pltpu.VMEM((tm + POOL_HALO, MIX), F32)],
        compiler_params=_params("parallel"),
        name="pool_mixer",
    )(c_in, c_in, w, scale.reshape(1, MIX))


def _sgu_kernel(d_ref, g_ref, b_ref, ws_ref, bs_ref, o_ref):
    causal = (lax.broadcasted_iota(jnp.int32, (SG_CHUNK, SG_CHUNK), 0)
              >= lax.broadcasted_iota(jnp.int32, (SG_CHUNK, SG_CHUNK), 1))
    hd = MIX // SG_H
    wm = [jnp.where(causal, ws_ref[g], 0.0).astype(BF16) for g in range(SG_H)]
    for c in range(d_ref.shape[0] // SG_CHUNK):
        rows = slice(c * SG_CHUNK, (c + 1) * SG_CHUNK)
        v = _ln(jax.nn.gelu(d_ref[rows, MIX:2 * MIX]), g_ref[...], b_ref[...]).astype(BF16)
        for g in range(SG_H):
            cols = slice(g * hd, (g + 1) * hd)
            vs = jnp.dot(wm[g], v[:, cols], preferred_element_type=F32) + bs_ref[:, g:g + 1]
            o_ref[rows, cols] = (jax.nn.gelu(d_ref[rows, cols]) * vs).astype(o_ref.dtype)


def sgu_mixer(d_in, col, ln_g, ln_b, ws, bs, chunks_per_step=4):
    M = d_in.shape[0]
    tm = chunks_per_step * SG_CHUNK
    return pl.pallas_call(
        _sgu_kernel,
        grid=(M // tm,),
        in_specs=[pl.BlockSpec((tm, 2 * MIX), lambda i: (i, col)),
                  pl.BlockSpec((1, MIX), lambda i: (0, 0)),
                  pl.BlockSpec((1, MIX), lambda i: (0, 0)),
                  pl.BlockSpec((SG_H, SG_CHUNK, SG_CHUNK), lambda i: (0, 0, 0)),
                  pl.BlockSpec((SG_CHUNK, SG_H), lambda i: (0, 0))],
        out_specs=pl.BlockSpec((tm, MIX), lambda i: (i, 0)),
        out_shape=jax.ShapeDtypeStruct((M, MIX), BF16),
        compiler_params=_params("parallel"),
        name="sgu_mixer",
    )(d_in, ln_g.reshape(1, MIX), ln_b.reshape(1, MIX), ws, bs.T)


def _xattn_kernel(q_ref, k_ref, v_ref, o_ref):
    for h in range(X_HEADS):
        sl = slice(h * X_DH, (h + 1) * X_DH)
        s = lax.dot_general(q_ref[:, sl], k_ref[:, sl], (((1,), (1,)), ((), ())),
                            preferred_element_type=F32) * (X_DH ** -0.5)
        p = jnp.exp(s - jnp.max(s, axis=-1, keepdims=True))
        p = p / jnp.sum(p, axis=-1, keepdims=True)
        o_ref[:, sl] = jnp.dot(p.astype(BF16), v_ref[:, sl], preferred_element_type=F32).astype(o_ref.dtype)


def cross_attention(q, k, v, S, mem_len, tm=512):
    M, W = q.shape
    return pl.pallas_call(
        _xattn_kernel,
        grid=(M // tm,),
        in_specs=[pl.BlockSpec((tm, W), lambda i: (i, 0)),
                  pl.BlockSpec((mem_len, W), lambda i: (i * tm // S, 0)),
                  pl.BlockSpec((mem_len, W), lambda i: (i * tm // S, 0))],
        out_specs=pl.BlockSpec((tm, W), lambda i: (i, 0)),
        out_shape=jax.ShapeDtypeStruct((M, W), BF16),
        compiler_params=_params("parallel"),
        name="cross_attention",
    )(q, k, v)


def _ffn_in_kernel(x_ref, halo_ref, wg32_ref, wu32_ref, cw_ref, cb_ref, o_ref, wg_ref, wu_ref, *, tiles_per_seq):
    @pl.when(pl.program_id(1) == 0)
    def _():
        wg_ref[...] = wg32_ref[...].astype(BF16)
        wu_ref[...] = wu32_ref[...].astype(BF16)

    first = pl.program_id(1) % tiles_per_seq == 0
    x = x_ref[...]
    g_main = jnp.dot(x, wg_ref[...], preferred_element_type=F32)
    g_halo = jnp.dot(halo_ref[...], wg_ref[...], preferred_element_type=F32)
    g_halo = jnp.where(first, 0.0, g_halo)
    ext = jnp.concatenate([g_halo, g_main], axis=0)
    n = ext.shape[0]
    conv = cb_ref[...] + cw_ref[CONV_W - 1:CONV_W] * g_main
    for d in range(1, CONV_W):
        conv = conv + cw_ref[CONV_W - 1 - d:CONV_W - d] * pltpu.roll(ext, d, axis=0)[CONV_HALO:n]
    up = jnp.dot(x, wu_ref[...], preferred_element_type=F32)
    o_ref[...] = (conv * jax.nn.sigmoid(conv) * up).astype(o_ref.dtype)


def ffn_in(xbf, w_in_all, conv_w_all, conv_b_all, layer, S, tm=1024, tn=512):
    M, D = xbf.shape
    F = w_in_all.shape[2] // 2
    nj = F // tn
    return pl.pallas_call(
        functools.partial(_ffn_in_kernel, tiles_per_seq=S // tm),
        grid=(nj, M // tm),
        in_specs=[pl.BlockSpec((tm, D), lambda j, i: (i, 0)),
                  pl.BlockSpec((CONV_HALO, D), lambda j, i: (jnp.maximum(i * (tm // CONV_HALO) - 1, 0), 0)),
                  pl.BlockSpec((None, D, tn), lambda j, i: (layer, 0, j)),
                  pl.BlockSpec((None, D, tn), lambda j, i: (layer, 0, j + nj)),
                  pl.BlockSpec((None, CONV_W, tn), lambda j, i: (layer, 0, j)),
                  pl.BlockSpec((None, 1, tn), lambda j, i: (layer, 0, j))],
        out_specs=pl.BlockSpec((tm, tn), lambda j, i: (i, j)),
        out_shape=jax.ShapeDtypeStruct((M, F), BF16),
        scratch_shapes=[pltpu.VMEM((D, tn), BF16), pltpu.VMEM((D, tn), BF16)],
        compiler_params=_params("parallel", "arbitrary"),
        name="ffn_in",
    )(xbf, xbf, w_in_all, w_in_all, conv_w_all, conv_b_all.reshape(conv_b_all.shape[0], 1, F))


def _nsa_weights(w_q, w_kv, w_gate):
    D = w_q.shape[0]
    gap = jnp.zeros((D, V7X_LANES - NSA_DH), F32)
    per_branch = 2 * NSA_G * NSA_DH
    k_cols, v_cols = [], []
    for br in (1, 2):
        base = br * per_branch
        for g in range(NSA_G):
            k_cols += [w_kv[:, base + g * NSA_DH:base + (g + 1) * NSA_DH], gap]
        v_cols.append(w_kv[:, base + NSA_G * NSA_DH:base + per_branch])
    gate = jnp.pad(w_gate, ((0, 0), (0, V7X_LANES - w_gate.shape[1])))
    return jnp.concatenate([w_q, gate, w_kv[:, :per_branch]] + k_cols + v_cols, axis=1).astype(BF16)


def _compress_weights(cmp_pos, cmp_w):
    half = NSA_CMP_LEN // 2
    eye_g = jnp.eye(NSA_G, dtype=F32)
    ws, ps = [], []
    for part in range(2):
        blocks = []
        for kv in range(2):
            w = cmp_w[kv, part * half:(part + 1) * half]
            full = jnp.einsum('lde,gh->lgdhe', w, eye_g)
            z = jnp.zeros_like(full)
            pair = [full, z] if kv == 0 else [z, full]
            blocks.append(jnp.stack(pair, axis=3))
        w_all = jnp.stack(blocks, axis=1)
        w_all = w_all.reshape(half * 2 * NSA_G * NSA_DH, 2 * NSA_G * NSA_DH)
        gap = jnp.zeros((w_all.shape[0], V7X_LANES - NSA_DH), F32)
        w_all = jnp.concatenate([w_all[:, :NSA_DH], gap, w_all[:, NSA_DH:2 * NSA_DH], gap,
                                 w_all[:, 2 * NSA_DH:]], axis=1)
        ws.append(w_all.astype(BF16))
        p = cmp_pos[:, part * half:(part + 1) * half]
        p = jnp.broadcast_to(p.transpose(1, 0, 2)[:, :, None, :], (half, 2, NSA_G, NSA_DH))
        ps.append(p.reshape(-1))
    return jnp.stack(ps, axis=0), ws[0], ws[1]


def kernel(x, mem, w_in, nsa_cmp_pos, nsa_cmp_w, hgrn_lb_logits, hgrn_norm_g, pool_w, pool_scale, sg_ln_g, sg_ln_b, sg_w, sg_b, w_branch, w_mix_out, ln_mix_g, ln_mix_b, mem_ln_g, mem_ln_b, xattn_q, xattn_k, xattn_v, xattn_o, ln_x_g, ln_x_b, ffn_in_w, ffn_conv_w, ffn_conv_b, ffn_out, ln_ffn_g, ln_ffn_b):
    B, S, D = x.shape
    M = B * S
    mem_len = mem.shape[1]
    assert D == D_MODEL and S % 512 == 0 and S // NSA_SEL_BLOCK <= V7X_LANES

    sm = jax.nn.softmax(hgrn_lb_logits.astype(F32), axis=0)
    lower_bounds = jnp.cumsum(sm, axis=0) - sm[0]
    sel_map_t = _sel_map_t(S)
    mem_n = layer_norm_rows(mem.reshape(B * mem_len, D), mem_ln_g, mem_ln_b, BF16)

    xf = x.reshape(M, D)
    xbf = xf.astype(BF16)
    o0, o1, o2, o3, o4, o5 = np.cumsum([MIX, 768, 24, 4 * MIX, MIX, 2 * MIX])
    for l in range(DEPTH):
        w = w_in[l]
        w_nsa = _nsa_weights(w[:, :o0], w[:, o0:o1], w[:, o1:o2])
        w_bdc = jnp.concatenate([w[:, o2:o3], w[:, o4:o5], w[:, o3:o4]], axis=1).astype(BF16)
        w_merge = w[:, o5:].astype(BF16)

        a_q, a_g, kv_cmp, k_sw, vt_sw = nsa_project(xbf, w_nsa, S)
        bdc = matmul_resident(xbf, w_bdc, F32)

        pos2, wlo, whi = _compress_weights(nsa_cmp_pos[l], nsa_cmp_w[l])
        kc, vct = nsa_compress(kv_cmp, pos2, wlo, whi, B, S)
        o_a = nsa_attention(a_q, a_g, k_sw, vt_sw, kc, vct, sel_map_t, B, S)
        o_b = hgrn2(bdc, lower_bounds[l], hgrn_norm_g[l], B, S)
        o_c = pool_mixer(bdc, 6, pool_w[l].astype(BF16), pool_scale[l], S)
        o_d = sgu_mixer(bdc, 2, sg_ln_g[l], sg_ln_b[l], sg_w[l], sg_b[l])

        gated = merge_branches(xbf, (o_a, o_b, o_c, o_d), w_merge, w_branch[l].astype(BF16))
        xf, xbf = matmul_residual_ln(gated, w_mix_out[l].astype(BF16), xf, ln_mix_g[l], ln_mix_b[l])

        xq = matmul_resident(xbf, xattn_q[l].astype(BF16), BF16, tm=512)
        xk = matmul(mem_n, xattn_k[l].astype(BF16), BF16)
        xv = matmul(mem_n, xattn_v[l].astype(BF16), BF16)
        xo = cross_attention(xq, xk, xv, S, mem_len)
        xf, xbf = matmul_residual_ln(xo, xattn_o[l].astype(BF16), xf, ln_x_g[l], ln_x_b[l])

        act = ffn_in(xbf, ffn_in_w, ffn_conv_w, ffn_conv_b, l, S)
        xf, xbf = matmul_residual_ln(act, ffn_out[l].astype(BF16), xf, ln_ffn_g[l], ln_ffn_b[l])
    return xf.reshape(B, S, D)
```
